```python
import math
import jax
import jax.numpy as jnp
from jax import lax
import numpy as np

D_MODEL = 1024
BATCH = 8
SEQ = 2048
DEPTH = 4
DEC_BATCH = 128
DEC_SEQ = 4
PAST_LEN = 2048
PAGE_SIZE = 128

N_MIXERS = 3
N_S5 = len(range(0, DEPTH, N_MIXERS))
N_MOBA = len(range(1, DEPTH, N_MIXERS))
N_FOX = len(range(2, DEPTH, N_MIXERS))

S5_WIDTH = D_MODEL
S5_GROUP = 16
S5_GROUPS = S5_WIDTH // S5_GROUP
S5_STATE = 64
N_HEADS = 16
HEAD_DIM = D_MODEL // N_HEADS
ATT_WIDTH = N_HEADS * HEAD_DIM
ROPE_DIM = HEAD_DIM // 4
ROPE_THETA = 500000.0
MOBA_BLOCK = 256
MOBA_TOPK = 3
MOBA_Q_CHUNK = 64
FOX_Q_BLOCK = 128
EPS = 1e-6
NEG_INF = -1e30

kernel_name = 'hybrid_s5_moba_fox_decode_step'


def rmsnorm(x, g):
    xf = x.astype(jnp.float32)
    y = xf * lax.rsqrt(jnp.mean(xf * xf, axis=-1, keepdims=True) + EPS)
    return (y * g.astype(jnp.float32)).astype(x.dtype)


def partial_rotary(x, pos):
    half = ROPE_DIM // 2
    inv_freq = ROPE_THETA ** (-jnp.arange(half, dtype=jnp.float32) / half)
    ang = pos.astype(jnp.float32)[:, None] * inv_freq[None, :]
    cos = jnp.cos(ang)[None, :, None, :]
    sin = jnp.sin(ang)[None, :, None, :]
    xr = x[..., :ROPE_DIM].astype(jnp.float32)
    x1, x2 = xr[..., :half], xr[..., half:]
    rot = jnp.concatenate([x1 * cos - x2 * sin, x2 * cos + x1 * sin], axis=-1).astype(x.dtype)
    return jnp.concatenate([rot, x[..., ROPE_DIM:]], axis=-1)


def paged_rows(cache, page_table):
    g = cache[page_table]
    return g.reshape((g.shape[0], g.shape[1] * g.shape[2]) + g.shape[3:])


def s5_discretize(lam_re, lam_im, b_re, b_im, log_dt):
    f32 = jnp.float32
    dt = jnp.exp(log_dt.astype(f32))[:, None]
    lr, li = lam_re.astype(f32), lam_im.astype(f32)
    mag = jnp.exp(lr * dt)
    ab_re, ab_im = mag * jnp.cos(li * dt), mag * jnp.sin(li * dt)
    nr, ni = ab_re - 1.0, ab_im
    den = lr * lr + li * li
    coef_re = (nr * lr + ni * li) / den
    coef_im = (ni * lr - nr * li) / den
    br, bi = b_re.astype(f32), b_im.astype(f32)
    bb_re = coef_re[..., None] * br - coef_im[..., None] * bi
    bb_im = coef_re[..., None] * bi + coef_im[..., None] * br
    return ab_re, ab_im, bb_re, bb_im


def _ssm_combine(e1, e2):
    a1r, a1i, b1r, b1i = e1
    a2r, a2i, b2r, b2i = e2
    return (a2r * a1r - a2i * a1i, a2r * a1i + a2i * a1r,
            a2r * b1r - a2i * b1i + b2r, a2r * b1i + a2i * b1r + b2i)


def s5_mix(u, h0_re, h0_im, lam_re, lam_im, b_re, b_im, c_re, c_im, d_skip, log_dt):
    bn, t, w = u.shape
    uf = u.astype(jnp.float32)
    ug = uf.reshape(bn, t, S5_GROUPS, S5_GROUP)
    ab_re, ab_im, bb_re, bb_im = s5_discretize(lam_re, lam_im, b_re, b_im, log_dt)
    bu_re = jnp.einsum('btgc,gpc->btgp', ug, bb_re)
    bu_im = jnp.einsum('btgc,gpc->btgp', ug, bb_im)
    bu_re = bu_re.at[:, 0].add(ab_re * h0_re - ab_im * h0_im)
    bu_im = bu_im.at[:, 0].add(ab_re * h0_im + ab_im * h0_re)
    a_re = jnp.broadcast_to(ab_re, bu_re.shape)
    a_im = jnp.broadcast_to(ab_im, bu_im.shape)
    _, _, h_re, h_im = lax.associative_scan(_ssm_combine, (a_re, a_im, bu_re, bu_im), axis=1)
    y = (jnp.einsum('btgp,gcp->btgc', h_re, c_re.astype(jnp.float32))
         - jnp.einsum('btgp,gcp->btgc', h_im, c_im.astype(jnp.float32)))
    y = y.reshape(bn, t, w) + d_skip.astype(jnp.float32) * uf
    return y, h_re[:, -1], h_im[:, -1]


def s5_layer(a, h0_re, h0_im, w_in, lam_re, lam_im, b_re, b_im, c_re, c_im, d_skip, log_dt,
             w_glu, b_glu, w_out):
    u, gate = jnp.split(a @ w_in, 2, axis=-1)
    y, h_re, h_im = s5_mix(u, h0_re, h0_im, lam_re, lam_im, b_re, b_im, c_re, c_im, d_skip, log_dt)
    g = jax.nn.gelu(y.astype(u.dtype))
    z = g * jax.nn.sigmoid(g @ w_glu + b_glu)
    return (z * jax.nn.silu(gate)) @ w_out, h_re, h_im


def _moba_one_seq(q, k, v, q_pos):
    tq, h, dh = q.shape
    tk = k.shape[0]
    nb = -(-tk // MOBA_BLOCK)
    pad = nb * MOBA_BLOCK - tk
    kb = jnp.pad(k, ((0, pad), (0, 0), (0, 0))).reshape(nb, MOBA_BLOCK, h, dh).transpose(2, 0, 1, 3)
    vb = jnp.pad(v, ((0, pad), (0, 0), (0, 0))).reshape(nb, MOBA_BLOCK, h, dh).transpose(2, 0, 1, 3)
    k_mean = jnp.mean(kb.astype(jnp.float32), axis=2)
    n_sel = min(MOBA_TOPK, nb)
    blk_ids = jnp.arange(nb)
    offs = jnp.arange(MOBA_BLOCK)
    head_ids = jnp.arange(h)[:, None, None]
    scale = dh ** -0.5
    qc_len = math.gcd(tq, MOBA_Q_CHUNK)
    nc = tq // qc_len

    def chunk(args):
        qc, pc = args
        own = pc // MOBA_BLOCK
        gate = jnp.einsum('qhd,hnd->hqn', qc.astype(jnp.float32), k_mean)
        gate = jnp.where(blk_ids[None, None, :] < own[None, :, None], gate, NEG_INF)
        _, top = lax.top_k(gate, n_sel)
        valid = top < own[None, :, None]
        sel = jnp.concatenate([top, jnp.broadcast_to(own[None, :, None], (h, qc_len, 1))], axis=-1)
        ok = jnp.concatenate([valid, jnp.ones((h, qc_len, 1), dtype=bool)], axis=-1)
        k_sel = kb[head_ids, sel]
        v_sel = vb[head_ids, sel]
        kpos = sel[..., None] * MOBA_BLOCK + offs
        mask = ok[..., None] & (kpos <= pc[None, :, None, None])
        s = jnp.einsum('qhd,hqjpd->hqjp', qc, k_sel, preferred_element_type=jnp.float32) * scale
        s = jnp.where(mask, s, NEG_INF).reshape(h, qc_len, -1)
        p = jax.nn.softmax(s, axis=-1).reshape(mask.shape)
        return jnp.einsum('hqjp,hqjpd->qhd', p.astype(v.dtype), v_sel)

    out = lax.map(chunk, (q.reshape(nc, qc_len, h, dh), q_pos.reshape(nc, qc_len)))
    return out.reshape(tq, h, dh)


def moba_layer(a, pos, k_past, v_past, w_in, w_out):
    bn, t, _ = a.shape
    q, k, v, gate = jnp.split(a @ w_in, 4, axis=-1)
    q = partial_rotary(q.reshape(bn, t, N_HEADS, HEAD_DIM), pos)
    k = partial_rotary(k.reshape(bn, t, N_HEADS, HEAD_DIM), pos)
    v = v.reshape(bn, t, N_HEADS, HEAD_DIM)
    if k_past is None:
        k_all, v_all = k, v
    else:
        k_all = jnp.concatenate([k_past.astype(k.dtype), k], axis=1)
        v_all = jnp.concatenate([v_past.astype(v.dtype), v], axis=1)
    o = lax.map(lambda e: _moba_one_seq(e[0], e[1], e[2], pos), (q, k_all, v_all))
    y = (o.reshape(bn, t, ATT_WIDTH) * jax.nn.silu(gate)) @ w_out
    return y, k, v


def fox_attend(q, k, v, c_q, c_k, q_pos):
    bn, tq, h, dh = q.shape
    tk = k.shape[1]
    qb = math.gcd(tq, FOX_Q_BLOCK)
    nq = tq // qb
    k_pos = jnp.arange(tk)
    c_kt = c_k.transpose(0, 2, 1)
    scale = dh ** -0.5

    def block(args):
        qblk, cblk, pblk = args
        s = jnp.einsum('bqhd,bkhd->bhqk', qblk, k, preferred_element_type=jnp.float32) * scale
        s = s + cblk.transpose(0, 2, 1)[..., None] - c_kt[:, :, None, :]
        s = jnp.where(k_pos[None, :] <= pblk[:, None], s, NEG_INF)
        p = jax.nn.softmax(s, axis=-1)
        return jnp.einsum('bhqk,bkhd->bqhd', p.astype(v.dtype), v)

    qs = q.reshape(bn, nq, qb, h, dh).swapaxes(0, 1)
    cs = c_q.reshape(bn, nq, qb, h).swapaxes(0, 1)
    out = lax.map(block, (qs, cs, q_pos.reshape(nq, qb)))
    return out.swapaxes(0, 1).reshape(bn, tq, h, dh)


def fox_layer(a, pos, k_past, v_past, logf_past, w_in, b_f, w_out):
    bn, t, _ = a.shape
    proj = a @ w_in
    q = proj[..., :ATT_WIDTH].reshape(bn, t, N_HEADS, HEAD_DIM)
    k = proj[..., ATT_WIDTH:2 * ATT_WIDTH].reshape(bn, t, N_HEADS, HEAD_DIM)
    v = proj[..., 2 * ATT_WIDTH:3 * ATT_WIDTH].reshape(bn, t, N_HEADS, HEAD_DIM)
    gate = proj[..., 3 * ATT_WIDTH:4 * ATT_WIDTH]
    logf = jax.nn.log_sigmoid(proj[..., 4 * ATT_WIDTH:].astype(jnp.float32)
                              + b_f.astype(jnp.float32))
    if k_past is None:
        k_all, v_all, logf_all = k, v, logf
    else:
        k_all = jnp.concatenate([k_past.astype(k.dtype), k], axis=1)
        v_all = jnp.concatenate([v_past.astype(v.dtype), v], axis=1)
        logf_all = jnp.concatenate([logf_past.astype(jnp.float32), logf], axis=1)
    c_all = jnp.cumsum(logf_all, axis=1)
    o = fox_attend(q, k_all, v_all, c_all[:, -t:], c_all, pos)
    y = (o.reshape(bn, t, ATT_WIDTH) * jax.nn.silu(gate)) @ w_out
    return y, k, v, logf


def setup_inputs(seed: int = 0) -> dict:
    key = jax.random.key(seed)
    ks = jax.random.split(key, 32)
    f32 = jnp.float32
    n_pages = PAST_LEN // PAGE_SIZE
    n_used = DEC_BATCH * n_pages
    n_pool = n_used + (n_used + 3) // 4

    def nrm(i, shape, scale):
        return scale * jax.random.normal(ks[i], shape, f32)

    page_table = jax.random.permutation(ks[0], n_pool)[:n_used].reshape(DEC_BATCH, n_pages).astype(jnp.int32)
    lam_im0 = jnp.pi * jnp.arange(S5_STATE, dtype=f32)
    kv_shape_b = (N_MOBA, n_pool, PAGE_SIZE, N_HEADS, HEAD_DIM)
    kv_shape_c = (N_FOX, n_pool, PAGE_SIZE, N_HEADS, HEAD_DIM)
    st_shape = (N_S5, DEC_BATCH, S5_GROUPS, S5_STATE)
    return {
        'x_prompt': nrm(1, (BATCH, SEQ, D_MODEL), 1.0),
        'x_sample': nrm(2, (DEC_BATCH, DEC_SEQ, D_MODEL), 1.0),
        'state_s5_re': nrm(3, st_shape, 0.5),
        'state_s5_im': nrm(4, st_shape, 0.5),
        'cache_moba_k': nrm(5, kv_shape_b, 1.0),
        'cache_moba_v': nrm(6, kv_shape_b, 1.0),
        'cache_fox_k': nrm(7, kv_shape_c, 1.0),
        'cache_fox_v': nrm(8, kv_shape_c, 1.0),
        'cache_fox_logf': jax.nn.log_sigmoid(2.0 + nrm(9, (N_FOX, n_pool, PAGE_SIZE, N_HEADS), 1.0)),
        'page_table': page_table,
        'norm_g': 1.0 + nrm(10, (DEPTH, D_MODEL), 0.01),
        'final_norm_g': 1.0 + nrm(11, (D_MODEL,), 0.01),
        's5_w_in': nrm(12, (N_S5, D_MODEL, 2 * S5_WIDTH), D_MODEL ** -0.5),
        's5_lam_re': -0.5 + nrm(13, (N_S5, S5_GROUPS, S5_STATE), 0.01),
        's5_lam_im': lam_im0 + nrm(14, (N_S5, S5_GROUPS, S5_STATE), 0.01),
        's5_b_re': nrm(15, (N_S5, S5_GROUPS, S5_STATE, S5_GROUP), (2 * S5_GROUP) ** -0.5),
        's5_b_im': nrm(16, (N_S5, S5_GROUPS, S5_STATE, S5_GROUP), (2 * S5_GROUP) ** -0.5),
        's5_c_re': nrm(17, (N_S5, S5_GROUPS, S5_GROUP, S5_STATE), (2 * S5_STATE) ** -0.5),
        's5_c_im': nrm(18, (N_S5, S5_GROUPS, S5_GROUP, S5_STATE), (2 * S5_STATE) ** -0.5),
        's5_d': nrm(19, (N_S5, S5_WIDTH), 0.5),
        's5_log_dt': jax.random.uniform(ks[20], (N_S5, S5_GROUPS), f32, math.log(1e-3), math.log(1e-1)),
        's5_w_glu': nrm(21, (N_S5, S5_WIDTH, S5_WIDTH), S5_WIDTH ** -0.5),
        's5_b_glu': nrm(22, (N_S5, S5_WIDTH), 0.01),
        's5_w_out': nrm(23, (N_S5, S5_WIDTH, D_MODEL), S5_WIDTH ** -0.5),
        'moba_w_in': nrm(24, (N_MOBA, D_MODEL, 4 * ATT_WIDTH), D_MODEL ** -0.5),
        'moba_w_out': nrm(25, (N_MOBA, ATT_WIDTH, D_MODEL), ATT_WIDTH ** -0.5),
        'fox_w_in': nrm(26, (N_FOX, D_MODEL, 4 * ATT_WIDTH + N_HEADS), D_MODEL ** -0.5),
        'fox_b_f': 2.0 + nrm(27, (N_FOX, N_HEADS), 0.1),
        'fox_w_out': nrm(28, (N_FOX, ATT_WIDTH, D_MODEL), ATT_WIDTH ** -0.5),
    }


def reference(x_prompt, x_sample, state_s5_re, state_s5_im, cache_moba_k, cache_moba_v,
              cache_fox_k, cache_fox_v, cache_fox_logf, page_table, norm_g, final_norm_g,
              s5_w_in, s5_lam_re, s5_lam_im, s5_b_re, s5_b_im, s5_c_re, s5_c_im, s5_d,
              s5_log_dt, s5_w_glu, s5_b_glu, s5_w_out, moba_w_in, moba_w_out,
              fox_w_in, fox_b_f, fox_w_out):
    seq = x_prompt.shape[1]
    dec_seq = x_sample.shape[1]
    past_len = page_table.shape[1] * cache_moba_k.shape[2]
    pos_p = jnp.arange(seq, dtype=jnp.int32)
    pos_s = past_len + jnp.arange(dec_seq, dtype=jnp.int32)
    hp, hs = x_prompt, x_sample
    s5r_p, s5i_p, s5r_s, s5i_s = [], [], [], []
    mk_p, mv_p, mk_s, mv_s = [], [], [], []
    fk_p, fv_p, fl_p, fk_s, fv_s, fl_s = [], [], [], [], [], []
    for i in range(DEPTH):
        kind, j = i % N_MIXERS, i // N_MIXERS
        an_p = rmsnorm(hp, norm_g[i])
        an_s = rmsnorm(hs, norm_g[i])
        if kind == 0:
            prm = (s5_w_in[j], s5_lam_re[j], s5_lam_im[j], s5_b_re[j], s5_b_im[j], s5_c_re[j],
                   s5_c_im[j], s5_d[j], s5_log_dt[j], s5_w_glu[j], s5_b_glu[j], s5_w_out[j])
            zero = jnp.zeros((hp.shape[0], S5_GROUPS, S5_STATE), jnp.float32)
            yp, re_p, im_p = s5_layer(an_p, zero, zero, *prm)
            ys, re_s, im_s = s5_layer(an_s, state_s5_re[j].astype(jnp.float32),
                                      state_s5_im[j].astype(jnp.float32), *prm)
            s5r_p.append(re_p); s5i_p.append(im_p); s5r_s.append(re_s); s5i_s.append(im_s)
        elif kind == 1:
            k_past = paged_rows(cache_moba_k[j], page_table)
            v_past = paged_rows(cache_moba_v[j], page_table)
            yp, kp, vp = moba_layer(an_p, pos_p, None, None, moba_w_in[j], moba_w_out[j])
            ys, ks_, vs_ = moba_layer(an_s, pos_s, k_past, v_past, moba_w_in[j], moba_w_out[j])
            mk_p.append(kp); mv_p.append(vp); mk_s.append(ks_); mv_s.append(vs_)
        else:
            k_past = paged_rows(cache_fox_k[j], page_table)
            v_past = paged_rows(cache_fox_v[j], page_table)
            lf_past = paged_rows(cache_fox_logf[j], page_table)
            yp, kp, vp, lp = fox_layer(an_p, pos_p, None, None, None, fox_w_in[j], fox_b_f[j], fox_w_out[j])
            ys, ks_, vs_, ls_ = fox_layer(an_s, pos_s, k_past, v_past, lf_past,
                                          fox_w_in[j], fox_b_f[j], fox_w_out[j])
            fk_p.append(kp); fv_p.append(vp); fl_p.append(lp)
            fk_s.append(ks_); fv_s.append(vs_); fl_s.append(ls_)
        hp = hp + yp
        hs = hs + ys
    y_prompt = rmsnorm(hp, final_norm_g)
    y_sample = rmsnorm(hs, final_norm_g)
    s5_re_prompt = jnp.stack(s5r_p)
    s5_im_prompt = jnp.stack(s5i_p)
    moba_k_prompt = jnp.stack(mk_p)
    moba_v_prompt = jnp.stack(mv_p)
    fox_k_prompt = jnp.stack(fk_p)
    fox_v_prompt = jnp.stack(fv_p)
    fox_logf_prompt = jnp.stack(fl_p)
    s5_re_sample = jnp.stack(s5r_s)
    s5_im_sample = jnp.stack(s5i_s)
    moba_k_sample = jnp.stack(mk_s)
    moba_v_sample = jnp.stack(mv_s)
    fox_k_sample = jnp.stack(fk_s)
    fox_v_sample = jnp.stack(fv_s)
    fox_logf_sample = jnp.stack(fl_s)
    return (y_prompt, y_sample,
            s5_re_prompt, s5_im_prompt, moba_k_prompt, moba_v_prompt,
            fox_k_prompt, fox_v_prompt, fox_logf_prompt,
            s5_re_sample, s5_im_sample, moba_k_sample, moba_v_sample,
            fox_k_sample, fox_v_sample, fox_logf_sample)
```

```python
import functools
import math

import jax
import jax.numpy as jnp
from jax import lax
from jax.experimental import pallas as pl
from jax.experimental.pallas import tpu as pltpu

F32 = jnp.float32
BF16 = jnp.bfloat16

D_MODEL = 1024
N_HEADS = 16
HEAD_DIM = 64
ROPE_DIM = 16
ROPE_HALF = ROPE_DIM // 2
ROPE_THETA = 500000.0
MOBA_BLOCK = 256
MOBA_TOPK = 3
S5_GROUPS = 64
S5_GROUP = 16
S5_STATE = 64
EPS = 1e-6
NEG_INF = -1e30
QK_SCALE = HEAD_DIM ** -0.5

LANES = 128
SUBLANES = 8
HEADS_PER_VREG = LANES // HEAD_DIM
N_HEAD_PAIRS = N_HEADS // HEADS_PER_VREG
N_SLABS = D_MODEL // LANES
GROUPS_PER_SLAB = LANES // S5_GROUP
SLAB_STATE = GROUPS_PER_SLAB * S5_STATE
VMEM_LIMIT = 52 * 1024 * 1024


def _cparams(n_axes):
    return pltpu.CompilerParams(dimension_semantics=("arbitrary",) * n_axes,
                                vmem_limit_bytes=VMEM_LIMIT)


def _rmsnorm(x, g):
    return x * lax.rsqrt(jnp.mean(x * x, axis=-1, keepdims=True) + EPS) * g


def _dot(a, b):
    return jnp.dot(a, b, preferred_element_type=F32)


def _dot_nt(a, b):
    return lax.dot_general(a, b, (((1,), (1,)), ((), ())), preferred_element_type=F32)


def _silu(x):
    return x * jax.nn.sigmoid(x)


def _gelu_tanh(x):
    return 0.5 * x * (1.0 + jnp.tanh(math.sqrt(2.0 / math.pi) * (x + 0.044715 * (x * x * x))))


def _cumsum_lanes(x, n):
    lane = lax.broadcasted_iota(jnp.int32, x.shape, x.ndim - 1)
    s = 1
    while s < n:
        x = x + jnp.where(lane >= s, pltpu.roll(x, s, x.ndim - 1), 0.0)
        s *= 2
    return x


def _s5_prep_body(lr_ref, li_ref, ldt_ref, br_ref, bi_ref, ar_ref, ai_ref, bbr_ref, bbi_ref):
    dt = jnp.exp(ldt_ref[...])
    lr, li = lr_ref[...], li_ref[...]
    mag = jnp.exp(lr * dt)
    ab_re, ab_im = mag * jnp.cos(li * dt), mag * jnp.sin(li * dt)
    nr, ni = ab_re - 1.0, ab_im
    den = lr * lr + li * li
    coef_re = (nr * lr + ni * li) / den
    coef_im = (ni * lr - nr * li) / den
    br, bi = br_ref[...], bi_ref[...]
    ar_ref[...] = ab_re
    ai_ref[...] = ab_im
    bbr_ref[...] = coef_re * br - coef_im * bi
    bbi_ref[...] = coef_re * bi + coef_im * br


def _s5_prep(lam_re, lam_im, log_dt, b_re, b_im):
    g, p, c = b_re.shape
    lr = lam_re.reshape(g, 1, p)
    li = lam_im.reshape(g, 1, p)
    ldt = jnp.broadcast_to(log_dt.reshape(g, 1, 1), (g, 1, p))
    br = b_re.transpose(0, 2, 1)
    bi = b_im.transpose(0, 2, 1)
    small = jax.ShapeDtypeStruct((g, 1, p), F32)
    big = jax.ShapeDtypeStruct((g, c, p), F32)
    return pl.pallas_call(_s5_prep_body, out_shape=(small, small, big, big), name="s5_prep")(
        lr, li, ldt, br, bi)


def _block_diag_slabs(m):
    g, c, p = m.shape
    m4 = m.reshape(N_SLABS, GROUPS_PER_SLAB, c, p)
    eye = jnp.eye(GROUPS_PER_SLAB, dtype=bool)[None, :, None, :, None]
    out = jnp.where(eye, m4[:, :, :, None, :], 0.0)
    return out.reshape(N_SLABS, GROUPS_PER_SLAB * c, GROUPS_PER_SLAB * p)


def _s5_layer_body(x_ref, g_ref, win_ref, bbd_ref, cre_ref, cim_ref, ar_ref, ai_ref, d_ref,
                   wglu_ref, bglu_ref, wout_ref, fg_ref, h0r_ref, h0i_ref,
                   o_ref, hr_ref, hi_ref,
                   proj_s, bu_s, hs_s, y_s, cr_s, ci_s,
                   *, time_major, final_norm):
    j = pl.program_id(1)
    nj = pl.num_programs(1)
    blk = x_ref.shape
    rows = blk[0] * blk[1]
    if time_major:
        tt, bt = blk[0], blk[1]
    else:
        bt, tt = blk[0], blk[1]
    nbs = bt // SUBLANES

    @pl.when(j == 0)
    def _():
        cr_s[...] = h0r_ref[...]
        ci_s[...] = h0i_ref[...]

    x = x_ref[...].reshape(rows, D_MODEL)
    a = _rmsnorm(x, g_ref[...]).astype(BF16)
    proj_s[...] = _dot(a, win_ref[...])
    ub = proj_s[:, :D_MODEL].astype(BF16)

    def at(ref, bs, t, lo, hi):
        b0 = bs * SUBLANES
        if time_major:
            return ref.at[pl.ds(t, 1), b0:b0 + SUBLANES, lo:hi]
        return ref.at[b0:b0 + SUBLANES, pl.ds(t, 1), lo:hi]

    shp = (1, SUBLANES, SLAB_STATE) if time_major else (SUBLANES, 1, SLAB_STATE)
    flat = (SUBLANES, SLAB_STATE)

    for s in range(N_SLABS):
        lo = s * LANES
        bu = _dot(ub[:, lo:lo + LANES], bbd_ref[s])
        bu_s[...] = bu.reshape(bu_s.shape)
        ar = jnp.broadcast_to(ar_ref[s], flat)
        ai = jnp.broadcast_to(ai_ref[s], flat)
        for bs in range(nbs):
            b0 = bs * SUBLANES
            st0 = s * SLAB_STATE

            def step(t, carry):
                h_re, h_im = carry
                bu_re = at(bu_s, bs, t, 0, SLAB_STATE)[...].reshape(flat)
                bu_im = at(bu_s, bs, t, SLAB_STATE, 2 * SLAB_STATE)[...].reshape(flat)
                n_re = ar * h_re - ai * h_im + bu_re
                n_im = ar * h_im + ai * h_re + bu_im
                at(hs_s, bs, t, 0, SLAB_STATE)[...] = n_re.reshape(shp)
                at(hs_s, bs, t, SLAB_STATE, 2 * SLAB_STATE)[...] = n_im.reshape(shp)
                return n_re, n_im

            h_re, h_im = lax.fori_loop(
                0, tt, step, (cr_s[b0:b0 + SUBLANES, st0:st0 + SLAB_STATE], ci_s[b0:b0 + SUBLANES, st0:st0 + SLAB_STATE]))
            cr_s[b0:b0 + SUBLANES, st0:st0 + SLAB_STATE] = h_re
            ci_s[b0:b0 + SUBLANES, st0:st0 + SLAB_STATE] = h_im
        hs = hs_s[...].reshape(rows, 2 * SLAB_STATE)
        y = (_dot(hs[:, :SLAB_STATE].astype(BF16), cre_ref[s])
             - _dot(hs[:, SLAB_STATE:].astype(BF16), cim_ref[s]))
        y_s[:, lo:lo + LANES] = y + d_ref[:, lo:lo + LANES] * proj_s[:, lo:lo + LANES]

    gl = _gelu_tanh(y_s[...])
    z = gl * jax.nn.sigmoid(_dot(gl.astype(BF16), wglu_ref[...]) + bglu_ref[...])
    v = (z * _silu(proj_s[:, D_MODEL:])).astype(BF16)
    out = x + _dot(v, wout_ref[...])
    if final_norm:
        out = _rmsnorm(out, fg_ref[...])
    o_ref[...] = out.reshape(blk)

    @pl.when(j == nj - 1)
    def _():
        hr_ref[...] = cr_s[...]
        hi_ref[...] = ci_s[...]


def _s5_layer(x, h0_re, h0_im, prm, *, time_major, tt, final_norm):
    if time_major:
        t_len, bn, _ = x.shape
        bt = bn
        blk = (tt, bt, D_MODEL)
        xmap = lambda i, j: (j, i, 0)
        scr3 = (tt, bt, 2 * SLAB_STATE)
    else:
        bn, t_len, _ = x.shape
        bt = SUBLANES
        blk = (bt, tt, D_MODEL)
        xmap = lambda i, j: (i, j, 0)
        scr3 = (bt, tt, 2 * SLAB_STATE)
    rows = bt * tt
    nstate = S5_GROUPS * S5_STATE
    const2 = lambda i, j: (0, 0)
    const3 = lambda i, j: (0, 0, 0)
    in_specs = [
        pl.BlockSpec(blk, xmap),
        pl.BlockSpec((1, D_MODEL), const2),
        pl.BlockSpec((D_MODEL, 2 * D_MODEL), const2),
        pl.BlockSpec((N_SLABS, LANES, 2 * SLAB_STATE), const3),
        pl.BlockSpec((N_SLABS, SLAB_STATE, LANES), const3),
        pl.BlockSpec((N_SLABS, SLAB_STATE, LANES), const3),
        pl.BlockSpec((N_SLABS, 1, SLAB_STATE), const3),
        pl.BlockSpec((N_SLABS, 1, SLAB_STATE), const3),
        pl.BlockSpec((1, D_MODEL), const2),
        pl.BlockSpec((D_MODEL, D_MODEL), const2),
        pl.BlockSpec((1, D_MODEL), const2),
        pl.BlockSpec((D_MODEL, D_MODEL), const2),
        pl.BlockSpec((1, D_MODEL), const2),
        pl.BlockSpec((bt, nstate), lambda i, j: (i, 0)),
        pl.BlockSpec((bt, nstate), lambda i, j: (i, 0)),
    ]
    out_specs = (
        pl.BlockSpec(blk, xmap),
        pl.BlockSpec((bt, nstate), lambda i, j: (i, 0)),
        pl.BlockSpec((bt, nstate), lambda i, j: (i, 0)),
    )
    out_shape = (
        jax.ShapeDtypeStruct(x.shape, F32),
        jax.ShapeDtypeStruct((bn, nstate), F32),
        jax.ShapeDtypeStruct((bn, nstate), F32),
    )
    scratch = [
        pltpu.VMEM((rows, 2 * D_MODEL), F32),
        pltpu.VMEM(scr3, F32),
        pltpu.VMEM(scr3, F32),
        pltpu.VMEM((rows, D_MODEL), F32),
        pltpu.VMEM((bt, nstate), F32),
        pltpu.VMEM((bt, nstate), F32),
    ]
    body = functools.partial(_s5_layer_body, time_major=time_major, final_norm=final_norm)
    return pl.pallas_call(
        body, grid=(bn // bt, t_len // tt), in_specs=in_specs, out_specs=out_specs,
        out_shape=out_shape, scratch_shapes=scratch, compiler_params=_cparams(2), name="s5_layer",
    )(x, prm["norm_g"], prm["w_in"], prm["bbd"], prm["c_re"], prm["c_im"], prm["ab_re"], prm["ab_im"],
      prm["d"], prm["w_glu"], prm["b_glu"], prm["w_out"], prm["final_g"], h0_re, h0_im)


def _s5_params(j, norm_g_i, final_norm_g, s5_w_in, s5_lam_re, s5_lam_im, s5_b_re, s5_b_im, s5_c_re,
               s5_c_im, s5_d, s5_log_dt, s5_w_glu, s5_b_glu, s5_w_out):
    ab_re, ab_im, bb_re, bb_im = _s5_prep(s5_lam_re[j], s5_lam_im[j], s5_log_dt[j], s5_b_re[j], s5_b_im[j])
    bbd = jnp.concatenate([_block_diag_slabs(bb_re), _block_diag_slabs(bb_im)], axis=-1).astype(BF16)
    c_re = _block_diag_slabs(s5_c_re[j]).transpose(0, 2, 1).astype(BF16)
    c_im = _block_diag_slabs(s5_c_im[j]).transpose(0, 2, 1).astype(BF16)
    return dict(
        norm_g=norm_g_i.reshape(1, D_MODEL), w_in=s5_w_in[j].astype(BF16), bbd=bbd, c_re=c_re, c_im=c_im,
        ab_re=ab_re.reshape(N_SLABS, 1, SLAB_STATE), ab_im=ab_im.reshape(N_SLABS, 1, SLAB_STATE),
        d=s5_d[j].reshape(1, D_MODEL), w_glu=s5_w_glu[j].astype(BF16), b_glu=s5_b_glu[j].reshape(1, D_MODEL),
        w_out=s5_w_out[j].astype(BF16), final_g=final_norm_g.reshape(1, D_MODEL))


def _rope_tables(pos):
    inv_freq = ROPE_THETA ** (-jnp.arange(ROPE_HALF, dtype=F32) / ROPE_HALF)
    ang = pos.astype(F32)[:, None] * inv_freq[None, :]
    cos, sin = jnp.cos(ang), jnp.sin(ang)
    r = pos.shape[0]
    ones = jnp.ones((r, HEAD_DIM - ROPE_DIM), F32)
    zeros = jnp.zeros((r, HEAD_DIM - ROPE_DIM), F32)
    zh = jnp.zeros((r, ROPE_HALF), F32)
    c = jnp.concatenate([cos, cos, ones], axis=1)
    sn = jnp.concatenate([-sin, zh, zeros], axis=1)
    sp = jnp.concatenate([zh, sin, zeros], axis=1)
    tile = lambda z: jnp.concatenate([z] * HEADS_PER_VREG, axis=1)
    return tile(c), tile(sn), tile(sp)


def _inproj_body(*refs, fox):
    if fox:
        x_ref, g_ref, w_ref, wf_ref, bf_ref, q_ref, k_ref, v_ref, gate_ref, lf_ref = refs
    else:
        x_ref, g_ref, w_ref, cos_ref, sn_ref, sp_ref, q_ref, k_ref, v_ref, gate_ref = refs
    a = _rmsnorm(x_ref[...], g_ref[...]).astype(BF16)
    proj = _dot(a, w_ref[...])
    if fox:
        q_ref[...] = proj[:, :D_MODEL]
        k_ref[...] = proj[:, D_MODEL:2 * D_MODEL]
        z = _dot_nt(wf_ref[...], a) + bf_ref[...]
        lf_ref[...] = jnp.minimum(z, 0.0) - jnp.log1p(jnp.exp(-jnp.abs(z)))
    else:
        cos, sn, sp = cos_ref[...], sn_ref[...], sp_ref[...]
        for dst, base in ((q_ref, 0), (k_ref, D_MODEL)):
            for c in range(D_MODEL // LANES):
                z = proj[:, base + c * LANES: base + (c + 1) * LANES]
                dst[:, c * LANES:(c + 1) * LANES] = (
                    z * cos + pltpu.roll(z, LANES - ROPE_HALF, 1) * sn + pltpu.roll(z, ROPE_HALF, 1) * sp)
    v_ref[...] = proj[:, 2 * D_MODEL:3 * D_MODEL]
    gate_ref[...] = proj[:, 3 * D_MODEL:]


def _inproj(x2d, norm_g, w4, tables=None, wf_t=None, b_f=None, *, tm=256):
    n = x2d.shape[0]
    fox = wf_t is not None
    row = lambda i: (i, 0)
    const = lambda i: (0, 0)
    in_specs = [pl.BlockSpec((tm, D_MODEL), row), pl.BlockSpec((1, D_MODEL), const),
                pl.BlockSpec((D_MODEL, 4 * D_MODEL), const)]
    out_specs = [pl.BlockSpec((tm, D_MODEL), row)] * 4
    out_shape = [jax.ShapeDtypeStruct((n, D_MODEL), F32)] * 4
    args = [x2d, norm_g.reshape(1, D_MODEL), w4]
    if fox:
        in_specs += [pl.BlockSpec((N_HEADS, D_MODEL), const), pl.BlockSpec((N_HEADS, 1), const)]
        out_specs += [pl.BlockSpec((N_HEADS, tm), lambda i: (0, i))]
        out_shape += [jax.ShapeDtypeStruct((N_HEADS, n), F32)]
        args += [wf_t, b_f.reshape(N_HEADS, 1)]
    else:
        nrep = tables[0].shape[0] // tm
        tab = lambda i: (i % nrep, 0)
        in_specs += [pl.BlockSpec((tm, LANES), tab)] * 3
        args += list(tables)
    return pl.pallas_call(
        functools.partial(_inproj_body, fox=fox), grid=(n // tm,), in_specs=in_specs,
        out_specs=tuple(out_specs), out_shape=tuple(out_shape), compiler_params=_cparams(1),
        name="inproj_fox" if fox else "inproj_moba")(*args)


def _outproj_body(o_ref, gate_ref, x_ref, w_ref, y_ref):
    v = (o_ref[...] * _silu(gate_ref[...])).astype(BF16)
    y_ref[...] = x_ref[...] + _dot(v, w_ref[...])


def _outproj(o2d, gate2d, x2d, w, *, tm=512):
    n = x2d.shape[0]
    row = lambda i: (i, 0)
    spec = pl.BlockSpec((tm, D_MODEL), row)
    return pl.pallas_call(
        _outproj_body, grid=(n // tm,),
        in_specs=[spec, spec, spec, pl.BlockSpec((D_MODEL, D_MODEL), lambda i: (0, 0))],
        out_specs=spec, out_shape=jax.ShapeDtypeStruct((n, D_MODEL), F32),
        compiler_params=_cparams(1), name="outproj")(o2d, gate2d, x2d, w)


def _cumsum_body(x_ref, o_ref):
    o_ref[...] = _cumsum_lanes(x_ref[...], x_ref.shape[-1])


def _cumsum(lf_t, t_len):
    h, n = lf_t.shape
    spec = pl.BlockSpec((h, t_len), lambda b: (0, b))
    return pl.pallas_call(_cumsum_body, grid=(n // t_len,), in_specs=[spec], out_specs=spec,
                          out_shape=jax.ShapeDtypeStruct((h, n), F32),
                          compiler_params=_cparams(1), name="cumsum")(lf_t)


def _head_mask(h):
    lane = lax.broadcasted_iota(jnp.int32, (1, LANES), 1)
    return (lane // HEAD_DIM) == h


def _attn_prompt_body(*refs, fox, tq):
    if fox:
        q_ref, k_ref, v_ref, ck_ref, o_ref, kb_s, vb_s, m_s, l_s, acc_s = refs
    else:
        q_ref, k_ref, v_ref, o_ref, kb_s, vb_s, km_s, m_s, l_s, acc_s = refs
    t_len = k_ref.shape[0]
    nb = t_len // tq
    qi = pl.program_id(2)

    @pl.when(qi == 0)
    def _():
        k = k_ref[...]
        kb_s[...] = k.astype(BF16)
        v = v_ref[...]
        for h in range(HEADS_PER_VREG):
            vb_s[h] = jnp.where(_head_mask(h), v, 0.0).astype(BF16)
        if not fox:
            for i in range(nb):
                km_s[i:i + 1, :] = jnp.mean(k[i * tq:(i + 1) * tq, :], axis=0, keepdims=True)

    q = q_ref[...]
    r0 = pl.multiple_of(qi * tq, tq)
    row_id = lax.broadcasted_iota(jnp.int32, (tq, tq), 0)
    col_id = lax.broadcasted_iota(jnp.int32, (tq, tq), 1)
    causal = col_id <= row_id
    outs = []
    for h in range(HEADS_PER_VREG):
        qm = jnp.where(_head_mask(h), q, 0.0)
        qh = (qm * QK_SCALE).astype(BF16)
        if not fox:
            gate = lax.dot_general(qm, km_s[...], (((1,), (1,)), ((), ())),
                                   precision=lax.Precision.HIGHEST, preferred_element_type=F32)
            blk_id = lax.broadcasted_iota(jnp.int32, (tq, nb), 1)
            gate = jnp.where(blk_id < qi, gate, NEG_INF)
            cnt = jnp.zeros((tq, nb), jnp.int32)
            for i in range(nb):
                col = gate[:, i:i + 1]
                beats = (col > gate) | ((col == gate) & (blk_id > i))
                cnt = cnt + beats.astype(jnp.int32)
            sel_bias = jnp.where((cnt < MOBA_TOPK) & (blk_id < qi), 0.0, NEG_INF)

        s = _dot_nt(qh, kb_s[pl.ds(r0, tq), :])
        if fox:
            s = s - ck_ref[h:h + 1, pl.ds(r0, tq)]
        s = jnp.where(causal, s, NEG_INF)
        m = jnp.max(s, axis=1, keepdims=True)
        p = jnp.exp(s - m)
        m_s[h] = m
        l_s[h] = jnp.sum(p, axis=1, keepdims=True)
        acc_s[h] = _dot(p.astype(BF16), vb_s[h, pl.ds(r0, tq), :])

        for i in range(nb - 1):
            @pl.when(i < qi)
            def _(i=i, h=h, qh=qh):
                s = _dot_nt(qh, kb_s[i * tq:(i + 1) * tq, :])
                if fox:
                    s = s - ck_ref[h:h + 1, i * tq:(i + 1) * tq]
                else:
                    s = s + sel_bias[:, i:i + 1]
                m_old = m_s[h]
                m_new = jnp.maximum(m_old, jnp.max(s, axis=1, keepdims=True))
                alpha = jnp.exp(m_old - m_new)
                p = jnp.exp(s - m_new)
                m_s[h] = m_new
                l_s[h] = alpha * l_s[h] + jnp.sum(p, axis=1, keepdims=True)
                acc_s[h] = alpha * acc_s[h] + _dot(p.astype(BF16), vb_s[h, i * tq:(i + 1) * tq, :])

        outs.append(acc_s[h] / l_s[h])
    o_ref[...] = outs[0] + outs[1]


def _attn_prompt(q2d, k2d, v2d, bn, t_len, ck=None, *, tq=MOBA_BLOCK):
    fox = ck is not None
    nq = t_len // tq
    qspec = pl.BlockSpec((tq, LANES), lambda b, hp, qi: (b * nq + qi, hp))
    kvspec = pl.BlockSpec((t_len, LANES), lambda b, hp, qi: (b, hp))
    in_specs = [qspec, kvspec, kvspec]
    args = [q2d, k2d, v2d]
    scratch = [pltpu.VMEM((t_len, LANES), BF16), pltpu.VMEM((HEADS_PER_VREG, t_len, LANES), BF16)]
    if fox:
        in_specs.append(pl.BlockSpec((None, None, HEADS_PER_VREG, t_len), lambda b, hp, qi: (b, hp, 0, 0)))
        args.append(ck)
    else:
        scratch.append(pltpu.VMEM((nq, LANES), F32))
    scratch += [pltpu.VMEM((HEADS_PER_VREG, tq, 1), F32), pltpu.VMEM((HEADS_PER_VREG, tq, 1), F32),
                pltpu.VMEM((HEADS_PER_VREG, tq, LANES), F32)]
    return pl.pallas_call(
        functools.partial(_attn_prompt_body, fox=fox, tq=tq), grid=(bn, N_HEAD_PAIRS, nq),
        in_specs=in_specs, out_specs=qspec, out_shape=jax.ShapeDtypeStruct(q2d.shape, F32),
        scratch_shapes=scratch, compiler_params=_cparams(3),
        name="attn_prompt_fox" if fox else "attn_prompt_moba")(*args)


def _attn_decode_body(*refs, fox, n_pages, tn, page):
    if fox:
        (pt_ref, q_ref, kn_ref, vn_ref, kc_ref, vc_ref, lfc_ref, lfn_ref, o_ref,
         qbd_s, m_s, l_s, r_s, acc_s, pad_s, c_s) = refs
    else:
        (pt_ref, q_ref, kn_ref, vn_ref, kc_ref, vc_ref, o_ref,
         qbd_s, m_s, l_s, r_s, acc_s, pad_s) = refs
    del pt_ref
    p_id = pl.program_id(1)
    nrow = tn * N_HEADS
    row_head = lax.broadcasted_iota(jnp.int32, (nrow, D_MODEL), 0) % N_HEADS
    lane_head = lax.broadcasted_iota(jnp.int32, (nrow, D_MODEL), 1) // HEAD_DIM
    head_mask = row_head == lane_head

    def rep_rows(z):
        return jnp.concatenate([jnp.broadcast_to(z[i:i + 1], (N_HEADS, z.shape[1])) for i in range(tn)], axis=0)

    @pl.when(p_id == 0)
    def _():
        qbd_s[...] = jnp.where(head_mask, rep_rows(q_ref[...]) * QK_SCALE, 0.0).astype(BF16)
        if fox:
            c_s[...] = jnp.zeros_like(c_s)

    def partial_softmax(s, vpage, idx):
        m = jnp.max(s, axis=1, keepdims=True)
        p = jnp.exp(s - m)
        m_s[idx] = m
        l_s[idx] = jnp.sum(p, axis=1, keepdims=True)
        acc_s[idx] = _dot(p.astype(BF16), vpage)

    s = _dot_nt(qbd_s[...], kc_ref[...].astype(BF16))
    r_s[p_id] = jnp.sum(s, axis=1, keepdims=True)
    if fox:
        c_loc = _cumsum_lanes(lfc_ref[...], page) + c_s[...]
        c_s[...] = jnp.broadcast_to(c_loc[:, page - 1:page], c_s.shape)
        s = s - jnp.concatenate([c_loc] * tn, axis=0)
    partial_softmax(s, vc_ref[...].astype(BF16), p_id)

    @pl.when(p_id == n_pages - 1)
    def _():
        pad_s[...] = jnp.zeros_like(pad_s)
        pad_s[0:tn, :] = kn_ref[...]
        s_new = _dot_nt(qbd_s[...], pad_s[...].astype(BF16))
        if fox:
            c_new = _cumsum_lanes(lfn_ref[...], page) + c_s[...]
            s_new = s_new - jnp.concatenate([c_new] * tn, axis=0)
        key_id = lax.broadcasted_iota(jnp.int32, (nrow, page), 1)
        q_id = lax.broadcasted_iota(jnp.int32, (nrow, page), 0) // N_HEADS
        s_new = jnp.where(key_id <= q_id, s_new, NEG_INF)
        pad_s[0:tn, :] = vn_ref[...]
        partial_softmax(s_new, pad_s[...].astype(BF16), n_pages)

        if fox:
            use = [None] * n_pages
        else:
            ppb = MOBA_BLOCK // page
            nblk = n_pages // ppb
            gates = []
            for b in range(nblk):
                g = r_s[b * ppb]
                for u in range(1, ppb):
                    g = g + r_s[b * ppb + u]
                gates.append(g)
            use = []
            for b in range(nblk):
                cnt = jnp.zeros((nrow, 1), jnp.int32)
                for b2 in range(nblk):
                    if b2 == b:
                        continue
                    beats = (gates[b2] > gates[b]) | ((gates[b2] == gates[b]) & (b2 < b))
                    cnt = cnt + beats.astype(jnp.int32)
                use += [cnt < MOBA_TOPK] * ppb

        m_tot = m_s[n_pages]
        for i in range(n_pages):
            mi = m_s[i] if use[i] is None else jnp.where(use[i], m_s[i], NEG_INF)
            m_tot = jnp.maximum(m_tot, mi)
        w_new = jnp.exp(m_s[n_pages] - m_tot)
        l_tot = w_new * l_s[n_pages]
        acc = w_new * acc_s[n_pages]
        for i in range(n_pages):
            w = jnp.exp(m_s[i] - m_tot)
            if use[i] is not None:
                w = jnp.where(use[i], w, 0.0)
            l_tot = l_tot + w * l_s[i]
            acc = acc + w * acc_s[i]
        o_full = jnp.where(head_mask, acc / l_tot, 0.0)
        for i in range(tn):
            o_ref[i:i + 1, :] = jnp.sum(o_full[i * N_HEADS:(i + 1) * N_HEADS, :], axis=0, keepdims=True)


def _attn_decode(q3, kn3, vn3, k_cache, v_cache, page_table, lf_cache_t=None, lf_new=None):
    fox = lf_cache_t is not None
    sn, tn, _ = q3.shape
    n_pages = page_table.shape[1]
    page = k_cache.shape[1]
    nrow = tn * N_HEADS
    seq = lambda s, p, pt: (s, 0, 0)
    pg = lambda s, p, pt: (pt[s * n_pages + p], 0, 0)
    new_spec = pl.BlockSpec((None, tn, D_MODEL), seq)
    cache_spec = pl.BlockSpec((None, page, D_MODEL), pg)
    in_specs = [new_spec, new_spec, new_spec, cache_spec, cache_spec]
    args = [q3, kn3, vn3, k_cache, v_cache]
    scratch = [pltpu.VMEM((nrow, D_MODEL), BF16),
               pltpu.VMEM((n_pages + 1, nrow, 1), F32), pltpu.VMEM((n_pages + 1, nrow, 1), F32),
               pltpu.VMEM((n_pages, nrow, 1), F32),
               pltpu.VMEM((n_pages + 1, nrow, D_MODEL), F32),
               pltpu.VMEM((page, D_MODEL), F32)]
    if fox:
        in_specs += [pl.BlockSpec((None, N_HEADS, page), pg), pl.BlockSpec((None, N_HEADS, page), seq)]
        args += [lf_cache_t, lf_new]
        scratch.append(pltpu.VMEM((N_HEADS, page), F32))
    grid_spec = pltpu.PrefetchScalarGridSpec(
        num_scalar_prefetch=1, grid=(sn, n_pages), in_specs=in_specs, out_specs=new_spec,
        scratch_shapes=scratch)
    return pl.pallas_call(
        functools.partial(_attn_decode_body, fox=fox, n_pages=n_pages, tn=tn, page=page),
        grid_spec=grid_spec, out_shape=jax.ShapeDtypeStruct(q3.shape, F32),
        compiler_params=_cparams(2),
        name="attn_decode_fox" if fox else "attn_decode_moba")(page_table.reshape(-1), *args)


def kernel(x_prompt, x_sample, state_s5_re, state_s5_im, cache_moba_k, cache_moba_v, cache_fox_k, cache_fox_v, cache_fox_logf, page_table, norm_g, final_norm_g, s5_w_in, s5_lam_re, s5_lam_im, s5_b_re, s5_b_im, s5_c_re, s5_c_im, s5_d, s5_log_dt, s5_w_glu, s5_b_glu, s5_w_out, moba_w_in, moba_w_out, fox_w_in, fox_b_f, fox_w_out):
    bn, t_len, _ = x_prompt.shape
    sn, tn, _ = x_sample.shape
    depth = norm_g.shape[0]
    n_pool, page = cache_moba_k.shape[1], cache_moba_k.shape[2]
    past_len = page_table.shape[1] * page
    nstate = S5_GROUPS * S5_STATE
    att = N_HEADS * HEAD_DIM

    tab_p = _rope_tables(jnp.arange(t_len, dtype=jnp.int32))
    pos_s = past_len + jnp.arange(tn, dtype=jnp.int32)
    tab_s = tuple(jnp.tile(z, (sn, 1)) for z in _rope_tables(pos_s))

    hp, hs = x_prompt, x_sample
    outs = {k: [] for k in ("s5r_p", "s5i_p", "s5r_s", "s5i_s", "mk_p", "mv_p", "mk_s", "mv_s",
                            "fk_p", "fv_p", "fl_p", "fk_s", "fv_s", "fl_s")}
    zero_state = jnp.zeros((bn, nstate), F32)
    for i in range(depth):
        kind, j = i % 3, i // 3
        last = i == depth - 1
        if kind == 0:
            prm = _s5_params(j, norm_g[i], final_norm_g, s5_w_in, s5_lam_re, s5_lam_im, s5_b_re, s5_b_im,
                             s5_c_re, s5_c_im, s5_d, s5_log_dt, s5_w_glu, s5_b_glu, s5_w_out)
            hp, re_p, im_p = _s5_layer(hp, zero_state, zero_state, prm, time_major=False, tt=32, final_norm=last)
            hs_t, re_s, im_s = _s5_layer(hs.transpose(1, 0, 2), state_s5_re[j].reshape(sn, nstate),
                                         state_s5_im[j].reshape(sn, nstate), prm,
                                         time_major=True, tt=tn, final_norm=last)
            hs = hs_t.transpose(1, 0, 2)
            outs["s5r_p"].append(re_p.reshape(bn, S5_GROUPS, S5_STATE))
            outs["s5i_p"].append(im_p.reshape(bn, S5_GROUPS, S5_STATE))
            outs["s5r_s"].append(re_s.reshape(sn, S5_GROUPS, S5_STATE))
            outs["s5i_s"].append(im_s.reshape(sn, S5_GROUPS, S5_STATE))
            continue

        fox = kind == 2
        w_in = (fox_w_in if fox else moba_w_in)[j]
        w_out = (fox_w_out if fox else moba_w_out)[j].astype(BF16)
        w4 = w_in[:, :4 * att].astype(BF16)
        xp2, xs2 = hp.reshape(bn * t_len, D_MODEL), hs.reshape(sn * tn, D_MODEL)
        if fox:
            wf_t = w_in[:, 4 * att:].T.astype(BF16)
            res_p = _inproj(xp2, norm_g[i], w4, wf_t=wf_t, b_f=fox_b_f[j])
            res_s = _inproj(xs2, norm_g[i], w4, wf_t=wf_t, b_f=fox_b_f[j])
        else:
            res_p = _inproj(xp2, norm_g[i], w4, tab_p)
            res_s = _inproj(xs2, norm_g[i], w4, tab_s)
        q_p, k_p, v_p, g_p = res_p[:4]
        q_s, k_s, v_s, g_s = res_s[:4]
        kc = (cache_fox_k if fox else cache_moba_k)[j].reshape(n_pool, page, D_MODEL)
        vc = (cache_fox_v if fox else cache_moba_v)[j].reshape(n_pool, page, D_MODEL)
        three = lambda z: z.reshape(sn, tn, D_MODEL)
        if fox:
            lf_p, lf_s = res_p[4], res_s[4]
            ck = _cumsum(lf_p, t_len).reshape(N_HEAD_PAIRS, HEADS_PER_VREG, bn, t_len).transpose(2, 0, 1, 3)
            o_p = _attn_prompt(q_p, k_p, v_p, bn, t_len, ck)
            lf_cache_t = cache_fox_logf[j].transpose(0, 2, 1)
            lf_new = jnp.pad(lf_s.reshape(N_HEADS, sn, tn).transpose(1, 0, 2), ((0, 0), (0, 0), (0, page - tn)))
            o_s = _attn_decode(three(q_s), three(k_s), three(v_s), kc, vc, page_table, lf_cache_t, lf_new)
            outs["fk_p"].append(k_p.reshape(bn, t_len, N_HEADS, HEAD_DIM))
            outs["fv_p"].append(v_p.reshape(bn, t_len, N_HEADS, HEAD_DIM))
            outs["fl_p"].append(lf_p.reshape(N_HEADS, bn, t_len).transpose(1, 2, 0))
            outs["fk_s"].append(k_s.reshape(sn, tn, N_HEADS, HEAD_DIM))
            outs["fv_s"].append(v_s.reshape(sn, tn, N_HEADS, HEAD_DIM))
            outs["fl_s"].append(lf_s.reshape(N_HEADS, sn, tn).transpose(1, 2, 0))
        else:
            o_p = _attn_prompt(q_p, k_p, v_p, bn, t_len)
            o_s = _attn_decode(three(q_s), three(k_s), three(v_s), kc, vc, page_table)
            outs["mk_p"].append(k_p.reshape(bn, t_len, N_HEADS, HEAD_DIM))
            outs["mv_p"].append(v_p.reshape(bn, t_len, N_HEADS, HEAD_DIM))
            outs["mk_s"].append(k_s.reshape(sn, tn, N_HEADS, HEAD_DIM))
            outs["mv_s"].append(v_s.reshape(sn, tn, N_HEADS, HEAD_DIM))
        hp = _outproj(o_p, g_p, xp2, w_out).reshape(bn, t_len, D_MODEL)
        hs = _outproj(o_s.reshape(sn * tn, D_MODEL), g_s, xs2, w_out).reshape(sn, tn, D_MODEL)

    st = lambda k: jnp.stack(outs[k])
    return (hp, hs, st("s5r_p"), st("s5i_p"), st("mk_p"), st("mv_p"), st("fk_p"), st("fv_p"), st("fl_p"),
            st("s5r_s"), st("s5i_s"), st("mk_s"), st("mv_s"), st("fk_s"), st("fv_s"), st("fl_s"))
```

```python
import functools
import math

import jax
import jax.numpy as jnp
from jax import lax
from jax.experimental import pallas as pl
from jax.experimental.pallas import tpu as pltpu

F32 = jnp.float32
BF16 = jnp.bfloat16

D_MODEL = 1024
N_HEADS = 16
HEAD_DIM = 64
ROPE_DIM = 16
ROPE_HALF = ROPE_DIM // 2
ROPE_THETA = 500000.0
MOBA_BLOCK = 256
MOBA_TOPK = 3
S5_GROUPS = 64
S5_GROUP = 16
S5_STATE = 64
EPS = 1e-6
NEG_INF = -1e30
QK_SCALE = HEAD_DIM ** -0.5

LANES = 128
SUBLANES = 8
HEADS_PER_VREG = LANES // HEAD_DIM
N_HEAD_PAIRS = N_HEADS // HEADS_PER_VREG
N_SLABS = D_MODEL // LANES
GROUPS_PER_SLAB = LANES // S5_GROUP
SLAB_STATE = GROUPS_PER_SLAB * S5_STATE
SCAN_SLABS = 2
VMEM_LIMIT = 52 * 1024 * 1024


def _cparams(n_axes):
    return pltpu.CompilerParams(dimension_semantics=("arbitrary",) * n_axes,
                                vmem_limit_bytes=VMEM_LIMIT)


def _rmsnorm(x, g):
    return x * lax.rsqrt(jnp.mean(x * x, axis=-1, keepdims=True) + EPS) * g


def _dot(a, b):
    return jnp.dot(a, b, preferred_element_type=F32)


def _dot_nt(a, b):
    return lax.dot_general(a, b, (((1,), (1,)), ((), ())), preferred_element_type=F32)


def _silu(x):
    return x * jax.nn.sigmoid(x)


def _gelu_tanh(x):
    return 0.5 * x * (1.0 + jnp.tanh(math.sqrt(2.0 / math.pi) * (x + 0.044715 * (x * x * x))))


def _cumsum_lanes(x, n):
    lane = lax.broadcasted_iota(jnp.int32, x.shape, x.ndim - 1)
    s = 1
    while s < n:
        x = x + jnp.where(lane >= s, pltpu.roll(x, s, x.ndim - 1), 0.0)
        s *= 2
    return x


def _s5_prep_body(lr_ref, li_ref, ldt_ref, br_ref, bi_ref, ar_ref, ai_ref, bbr_ref, bbi_ref):
    dt = jnp.exp(ldt_ref[...])
    lr, li = lr_ref[...], li_ref[...]
    mag = jnp.exp(lr * dt)
    ab_re, ab_im = mag * jnp.cos(li * dt), mag * jnp.sin(li * dt)
    nr, ni = ab_re - 1.0, ab_im
    den = lr * lr + li * li
    coef_re = (nr * lr + ni * li) / den
    coef_im = (ni * lr - nr * li) / den
    br, bi = br_ref[...], bi_ref[...]
    ar_ref[...] = ab_re
    ai_ref[...] = ab_im
    bbr_ref[...] = coef_re * br - coef_im * bi
    bbi_ref[...] = coef_re * bi + coef_im * br


def _s5_prep(lam_re, lam_im, log_dt, b_re, b_im):
    g, p, c = b_re.shape
    lr = lam_re.reshape(g, 1, p)
    li = lam_im.reshape(g, 1, p)
    ldt = jnp.broadcast_to(log_dt.reshape(g, 1, 1), (g, 1, p))
    br = b_re.transpose(0, 2, 1)
    bi = b_im.transpose(0, 2, 1)
    small = jax.ShapeDtypeStruct((g, 1, p), F32)
    big = jax.ShapeDtypeStruct((g, c, p), F32)
    return pl.pallas_call(_s5_prep_body, out_shape=(small, small, big, big), name="s5_prep")(
        lr, li, ldt, br, bi)


def _block_diag_slabs(m):
    g, c, p = m.shape
    m4 = m.reshape(N_SLABS, GROUPS_PER_SLAB, c, p)
    eye = jnp.eye(GROUPS_PER_SLAB, dtype=bool)[None, :, None, :, None]
    out = jnp.where(eye, m4[:, :, :, None, :], 0.0)
    return out.reshape(N_SLABS, GROUPS_PER_SLAB * c, GROUPS_PER_SLAB * p)


def _s5_layer_body(x_ref, g_ref, win_ref, bbd_ref, cre_ref, cim_ref, ar_ref, ai_ref, d_ref,
                   wglu_ref, bglu_ref, wout_ref, fg_ref, h0r_ref, h0i_ref,
                   o_ref, hr_ref, hi_ref,
                   x_s, proj_s, bu_s, hs_s, y_s, cr_s, ci_s,
                   *, time_major, final_norm):
    j = pl.program_id(1)
    nj = pl.num_programs(1)
    blk = x_ref.shape
    rows = blk[0] * blk[1]
    if time_major:
        tt, bt = blk[0], blk[1]
    else:
        bt, tt = blk[0], blk[1]
    nbs = bt // SUBLANES

    @pl.when(j == 0)
    def _():
        cr_s[...] = h0r_ref[...]
        ci_s[...] = h0i_ref[...]

    if time_major:
        x = x_ref[...].reshape(rows, D_MODEL)
    else:
        for t in range(tt):
            x_s[t * bt:(t + 1) * bt, :] = x_ref[:, t, :]
        x = x_s[...]
    a = _rmsnorm(x, g_ref[...]).astype(BF16)
    proj_s[...] = _dot(a, win_ref[...])
    ub = proj_s[:, :D_MODEL].astype(BF16)

    flat = (SUBLANES, SLAB_STATE)
    sw = 2 * SLAB_STATE
    for s0 in range(0, N_SLABS, SCAN_SLABS):
        slabs = range(s0, s0 + SCAN_SLABS)
        for u, s in enumerate(slabs):
            bu_s[:, u * sw:(u + 1) * sw] = _dot(ub[:, s * LANES:(s + 1) * LANES], bbd_ref[s])
        ar = [jnp.broadcast_to(ar_ref[s], flat) for s in slabs]
        ai = [jnp.broadcast_to(ai_ref[s], flat) for s in slabs]
        for bs in range(nbs):
            b0 = bs * SUBLANES

            def step(t, carry):
                r = pl.ds(pl.multiple_of(t * bt + b0, SUBLANES), SUBLANES)
                new = []
                for u in range(SCAN_SLABS):
                    h_re, h_im = carry[2 * u], carry[2 * u + 1]
                    re = slice(u * sw, u * sw + SLAB_STATE)
                    im = slice(u * sw + SLAB_STATE, (u + 1) * sw)
                    n_re = ar[u] * h_re - ai[u] * h_im + bu_s[r, re]
                    n_im = ar[u] * h_im + ai[u] * h_re + bu_s[r, im]
                    hs_s[r, re] = n_re
                    hs_s[r, im] = n_im
                    new += [n_re, n_im]
                return tuple(new)

            init = []
            for s in slabs:
                st0 = s * SLAB_STATE
                init += [cr_s[b0:b0 + SUBLANES, st0:st0 + SLAB_STATE], ci_s[b0:b0 + SUBLANES, st0:st0 + SLAB_STATE]]
            fin = lax.fori_loop(0, tt, step, tuple(init), unroll=2)
            for u, s in enumerate(slabs):
                st0 = s * SLAB_STATE
                cr_s[b0:b0 + SUBLANES, st0:st0 + SLAB_STATE] = fin[2 * u]
                ci_s[b0:b0 + SUBLANES, st0:st0 + SLAB_STATE] = fin[2 * u + 1]
        for u, s in enumerate(slabs):
            lo = s * LANES
            y = (_dot(hs_s[:, u * sw:u * sw + SLAB_STATE].astype(BF16), cre_ref[s])
                 - _dot(hs_s[:, u * sw + SLAB_STATE:(u + 1) * sw].astype(BF16), cim_ref[s]))
            y_s[:, lo:lo + LANES] = y + d_ref[:, lo:lo + LANES] * proj_s[:, lo:lo + LANES]

    gl = _gelu_tanh(y_s[...])
    z = gl * jax.nn.sigmoid(_dot(gl.astype(BF16), wglu_ref[...]) + bglu_ref[...])
    v = (z * _silu(proj_s[:, D_MODEL:])).astype(BF16)
    out = x + _dot(v, wout_ref[...])
    if final_norm:
        out = _rmsnorm(out, fg_ref[...])
    if time_major:
        o_ref[...] = out.reshape(blk)
    else:
        for t in range(tt):
            o_ref[:, t, :] = out[t * bt:(t + 1) * bt, :]

    @pl.when(j == nj - 1)
    def _():
        hr_ref[...] = cr_s[...]
        hi_ref[...] = ci_s[...]


def _s5_layer(x, h0_re, h0_im, prm, *, time_major, tt, final_norm):
    if time_major:
        t_len, bn, _ = x.shape
        bt = bn
        blk = (tt, bt, D_MODEL)
        xmap = lambda i, j: (j, i, 0)
    else:
        bn, t_len, _ = x.shape
        bt = SUBLANES
        blk = (bt, tt, D_MODEL)
        xmap = lambda i, j: (i, j, 0)
    rows = bt * tt
    nstate = S5_GROUPS * S5_STATE
    const2 = lambda i, j: (0, 0)
    const3 = lambda i, j: (0, 0, 0)
    in_specs = [
        pl.BlockSpec(blk, xmap),
        pl.BlockSpec((1, D_MODEL), const2),
        pl.BlockSpec((D_MODEL, 2 * D_MODEL), const2),
        pl.BlockSpec((N_SLABS, LANES, 2 * SLAB_STATE), const3),
        pl.BlockSpec((N_SLABS, SLAB_STATE, LANES), const3),
        pl.BlockSpec((N_SLABS, SLAB_STATE, LANES), const3),
        pl.BlockSpec((N_SLABS, 1, SLAB_STATE), const3),
        pl.BlockSpec((N_SLABS, 1, SLAB_STATE), const3),
        pl.BlockSpec((1, D_MODEL), const2),
        pl.BlockSpec((D_MODEL, D_MODEL), const2),
        pl.BlockSpec((1, D_MODEL), const2),
        pl.BlockSpec((D_MODEL, D_MODEL), const2),
        pl.BlockSpec((1, D_MODEL), const2),
        pl.BlockSpec((bt, nstate), lambda i, j: (i, 0)),
        pl.BlockSpec((bt, nstate), lambda i, j: (i, 0)),
    ]
    out_specs = (
        pl.BlockSpec(blk, xmap),
        pl.BlockSpec((bt, nstate), lambda i, j: (i, 0)),
        pl.BlockSpec((bt, nstate), lambda i, j: (i, 0)),
    )
    out_shape = (
        jax.ShapeDtypeStruct(x.shape, F32),
        jax.ShapeDtypeStruct((bn, nstate), F32),
        jax.ShapeDtypeStruct((bn, nstate), F32),
    )
    scratch = [
        pltpu.VMEM((rows, D_MODEL), F32),
        pltpu.VMEM((rows, 2 * D_MODEL), F32),
        pltpu.VMEM((rows, SCAN_SLABS * 2 * SLAB_STATE), F32),
        pltpu.VMEM((rows, SCAN_SLABS * 2 * SLAB_STATE), F32),
        pltpu.VMEM((rows, D_MODEL), F32),
        pltpu.VMEM((bt, nstate), F32),
        pltpu.VMEM((bt, nstate), F32),
    ]
    body = functools.partial(_s5_layer_body, time_major=time_major, final_norm=final_norm)
    return pl.pallas_call(
        body, grid=(bn // bt, t_len // tt), in_specs=in_specs, out_specs=out_specs,
        out_shape=out_shape, scratch_shapes=scratch, compiler_params=_cparams(2), name="s5_layer",
    )(x, prm["norm_g"], prm["w_in"], prm["bbd"], prm["c_re"], prm["c_im"], prm["ab_re"], prm["ab_im"],
      prm["d"], prm["w_glu"], prm["b_glu"], prm["w_out"], prm["final_g"], h0_re, h0_im)


def _s5_params(j, norm_g_i, final_norm_g, s5_w_in, s5_lam_re, s5_lam_im, s5_b_re, s5_b_im, s5_c_re,
               s5_c_im, s5_d, s5_log_dt, s5_w_glu, s5_b_glu, s5_w_out):
    ab_re, ab_im, bb_re, bb_im = _s5_prep(s5_lam_re[j], s5_lam_im[j], s5_log_dt[j], s5_b_re[j], s5_b_im[j])
    bbd = jnp.concatenate([_block_diag_slabs(bb_re), _block_diag_slabs(bb_im)], axis=-1).astype(BF16)
    c_re = _block_diag_slabs(s5_c_re[j]).transpose(0, 2, 1).astype(BF16)
    c_im = _block_diag_slabs(s5_c_im[j]).transpose(0, 2, 1).astype(BF16)
    return dict(
        norm_g=norm_g_i.reshape(1, D_MODEL), w_in=s5_w_in[j].astype(BF16), bbd=bbd, c_re=c_re, c_im=c_im,
        ab_re=ab_re.reshape(N_SLABS, 1, SLAB_STATE), ab_im=ab_im.reshape(N_SLABS, 1, SLAB_STATE),
        d=s5_d[j].reshape(1, D_MODEL), w_glu=s5_w_glu[j].astype(BF16), b_glu=s5_b_glu[j].reshape(1, D_MODEL),
        w_out=s5_w_out[j].astype(BF16), final_g=final_norm_g.reshape(1, D_MODEL))


def _rope_tables(pos):
    inv_freq = ROPE_THETA ** (-jnp.arange(ROPE_HALF, dtype=F32) / ROPE_HALF)
    ang = pos.astype(F32)[:, None] * inv_freq[None, :]
    cos, sin = jnp.cos(ang), jnp.sin(ang)
    r = pos.shape[0]
    ones = jnp.ones((r, HEAD_DIM - ROPE_DIM), F32)
    zeros = jnp.zeros((r, HEAD_DIM - ROPE_DIM), F32)
    zh = jnp.zeros((r, ROPE_HALF), F32)
    c = jnp.concatenate([cos, cos, ones], axis=1)
    sn = jnp.concatenate([-sin, zh, zeros], axis=1)
    sp = jnp.concatenate([zh, sin, zeros], axis=1)
    tile = lambda z: jnp.concatenate([z] * HEADS_PER_VREG, axis=1)
    return tile(c), tile(sn), tile(sp)


def _inproj_body(*refs, fox):
    if fox:
        x_ref, g_ref, w_ref, wf_ref, bf_ref, q_ref, k_ref, v_ref, gate_ref, lf_ref = refs
    else:
        x_ref, g_ref, w_ref, cos_ref, sn_ref, sp_ref, q_ref, k_ref, v_ref, gate_ref = refs
    a = _rmsnorm(x_ref[...], g_ref[...]).astype(BF16)
    proj = _dot(a, w_ref[...])
    if fox:
        q_ref[...] = proj[:, :D_MODEL]
        k_ref[...] = proj[:, D_MODEL:2 * D_MODEL]
        z = _dot_nt(wf_ref[...], a) + bf_ref[...]
        lf_ref[...] = jnp.minimum(z, 0.0) - jnp.log1p(jnp.exp(-jnp.abs(z)))
    else:
        cos, sn, sp = cos_ref[...], sn_ref[...], sp_ref[...]
        for dst, base in ((q_ref, 0), (k_ref, D_MODEL)):
            for c in range(D_MODEL // LANES):
                z = proj[:, base + c * LANES: base + (c + 1) * LANES]
                dst[:, c * LANES:(c + 1) * LANES] = (
                    z * cos + pltpu.roll(z, LANES - ROPE_HALF, 1) * sn + pltpu.roll(z, ROPE_HALF, 1) * sp)
    v_ref[...] = proj[:, 2 * D_MODEL:3 * D_MODEL]
    gate_ref[...] = proj[:, 3 * D_MODEL:]


def _inproj(x2d, norm_g, w4, tables=None, wf_t=None, b_f=None, *, tm=256):
    n = x2d.shape[0]
    fox = wf_t is not None
    row = lambda i: (i, 0)
    const = lambda i: (0, 0)
    in_specs = [pl.BlockSpec((tm, D_MODEL), row), pl.BlockSpec((1, D_MODEL), const),
                pl.BlockSpec((D_MODEL, 4 * D_MODEL), const)]
    out_specs = [pl.BlockSpec((tm, D_MODEL), row)] * 4
    out_shape = [jax.ShapeDtypeStruct((n, D_MODEL), F32)] * 4
    args = [x2d, norm_g.reshape(1, D_MODEL), w4]
    if fox:
        in_specs += [pl.BlockSpec((N_HEADS, D_MODEL), const), pl.BlockSpec((N_HEADS, 1), const)]
        out_specs += [pl.BlockSpec((N_HEADS, tm), lambda i: (0, i))]
        out_shape += [jax.ShapeDtypeStruct((N_HEADS, n), F32)]
        args += [wf_t, b_f.reshape(N_HEADS, 1)]
    else:
        nrep = tables[0].shape[0] // tm
        tab = lambda i: (i % nrep, 0)
        in_specs += [pl.BlockSpec((tm, LANES), tab)] * 3
        args += list(tables)
    return pl.pallas_call(
        functools.partial(_inproj_body, fox=fox), grid=(n // tm,), in_specs=in_specs,
        out_specs=tuple(out_specs), out_shape=tuple(out_shape), compiler_params=_cparams(1),
        name="inproj_fox" if fox else "inproj_moba")(*args)


def _outproj_body(o_ref, gate_ref, x_ref, w_ref, y_ref):
    v = (o_ref[...] * _silu(gate_ref[...])).astype(BF16)
    y_ref[...] = x_ref[...] + _dot(v, w_ref[...])


def _outproj(o2d, gate2d, x2d, w, *, tm=512):
    n = x2d.shape[0]
    row = lambda i: (i, 0)
    spec = pl.BlockSpec((tm, D_MODEL), row)
    return pl.pallas_call(
        _outproj_body, grid=(n // tm,),
        in_specs=[spec, spec, spec, pl.BlockSpec((D_MODEL, D_MODEL), lambda i: (0, 0))],
        out_specs=spec, out_shape=jax.ShapeDtypeStruct((n, D_MODEL), F32),
        compiler_params=_cparams(1), name="outproj")(o2d, gate2d, x2d, w)


def _cumsum_body(x_ref, o_ref):
    o_ref[...] = _cumsum_lanes(x_ref[...], x_ref.shape[-1])


def _cumsum(lf_t, t_len):
    h, n = lf_t.shape
    spec = pl.BlockSpec((h, t_len), lambda b: (0, b))
    return pl.pallas_call(_cumsum_body, grid=(n // t_len,), in_specs=[spec], out_specs=spec,
                          out_shape=jax.ShapeDtypeStruct((h, n), F32),
                          compiler_params=_cparams(1), name="cumsum")(lf_t)


def _head_mask(h):
    lane = lax.broadcasted_iota(jnp.int32, (1, LANES), 1)
    return (lane // HEAD_DIM) == h


def _attn_prompt_body(*refs, fox, tq):
    if fox:
        q_ref, k_ref, v_ref, ck_ref, o_ref, kb_s, vt_s, ckb_s = refs
    else:
        q_ref, k_ref, v_ref, o_ref, kb_s, vt_s, km_s = refs
    t_len = k_ref.shape[0]
    nb = t_len // tq

    k = k_ref[...]
    kb_s[...] = k.astype(BF16)
    for i in range(nb):
        vt_s[i] = v_ref[i * tq:(i + 1) * tq, :].T.astype(BF16)
        if not fox:
            km_s[i:i + 1, :] = jnp.mean(k[i * tq:(i + 1) * tq, :], axis=0, keepdims=True)
    if fox:
        for h in range(HEADS_PER_VREG):
            ckb_s[h] = jnp.broadcast_to(ck_ref[:, h:h + 1], (t_len, LANES))

    key_id = lax.broadcasted_iota(jnp.int32, (tq, tq), 0)
    qry_id = lax.broadcasted_iota(jnp.int32, (tq, tq), 1)
    causal = key_id <= qry_id
    blk_id = lax.broadcasted_iota(jnp.int32, (nb, tq), 0)
    for qi in range(nb):
        q = q_ref[qi * tq:(qi + 1) * tq, :]
        o_parts = []
        for h in range(HEADS_PER_VREG):
            hd = slice(h * HEAD_DIM, (h + 1) * HEAD_DIM)
            qm = jnp.where(_head_mask(h), q, 0.0)
            qt = (qm * QK_SCALE).T.astype(BF16)
            sel_bias = None
            if not fox and qi > MOBA_TOPK:
                gate = lax.dot_general(km_s[...], qm, (((1,), (1,)), ((), ())),
                                       precision=lax.Precision.HIGHEST, preferred_element_type=F32)
                gate = jnp.where(blk_id < qi, gate, NEG_INF)
                cnt = jnp.zeros((nb, tq), jnp.int32)
                for i in range(qi):
                    row = gate[i:i + 1, :]
                    beats = (row > gate) | ((row == gate) & (blk_id > i))
                    cnt = cnt + beats.astype(jnp.int32)
                sel_bias = jnp.where(cnt < MOBA_TOPK, 0.0, NEG_INF)

            def scores(i):
                s = _dot(kb_s[i * tq:(i + 1) * tq, :], qt)
                if fox:
                    b = ckb_s[h, i * tq:(i + 1) * tq, :]
                    s = s - jnp.concatenate([b] * (tq // LANES), axis=1)
                return s

            s = jnp.where(causal, scores(qi), NEG_INF)
            m = jnp.max(s, axis=0, keepdims=True)
            p = jnp.exp(s - m)
            l = jnp.sum(p, axis=0, keepdims=True)
            acc = _dot(vt_s[qi, hd, :], p.astype(BF16))
            for i in range(qi):
                s = scores(i)
                if sel_bias is not None:
                    s = s + sel_bias[i:i + 1, :]
                m_new = jnp.maximum(m, jnp.max(s, axis=0, keepdims=True))
                alpha = jnp.exp(m - m_new)
                p = jnp.exp(s - m_new)
                l = alpha * l + jnp.sum(p, axis=0, keepdims=True)
                acc = alpha * acc + _dot(vt_s[i, hd, :], p.astype(BF16))
                m = m_new
            o_parts.append(acc / l)
        o_ref[qi * tq:(qi + 1) * tq, :] = jnp.concatenate(o_parts, axis=0).T


def _attn_prompt(q2d, k2d, v2d, bn, t_len, ck=None, *, tq=MOBA_BLOCK):
    fox = ck is not None
    nq = t_len // tq
    spec = pl.BlockSpec((t_len, LANES), lambda b, hp: (b, hp))
    in_specs = [spec, spec, spec]
    args = [q2d, k2d, v2d]
    scratch = [pltpu.VMEM((t_len, LANES), BF16), pltpu.VMEM((nq, LANES, tq), BF16)]
    if fox:
        in_specs.append(pl.BlockSpec((None, None, t_len, HEADS_PER_VREG), lambda b, hp: (b, hp, 0, 0)))
        args.append(ck)
        scratch.append(pltpu.VMEM((HEADS_PER_VREG, t_len, LANES), F32))
    else:
        scratch.append(pltpu.VMEM((nq, LANES), F32))
    return pl.pallas_call(
        functools.partial(_attn_prompt_body, fox=fox, tq=tq), grid=(bn, N_HEAD_PAIRS),
        in_specs=in_specs, out_specs=spec, out_shape=jax.ShapeDtypeStruct(q2d.shape, F32),
        scratch_shapes=scratch, compiler_params=_cparams(2),
        name="attn_prompt_fox" if fox else "attn_prompt_moba")(*args)


def _cumsum_strided(x, stride, n):
    lane = lax.broadcasted_iota(jnp.int32, x.shape, 1)
    s = stride
    while s < n * stride:
        x = x + jnp.where(lane >= s, pltpu.roll(x, s, 1), 0.0)
        s *= 2
    return x


def _tile_last(x, stride, n):
    width = n * stride
    lane = lax.broadcasted_iota(jnp.int32, x.shape, 1)
    z = jnp.where(lane >= width - stride, x, 0.0)
    s = stride
    while s < width:
        z = z + pltpu.roll(z, width - s, 1)
        s *= 2
    return z


def _attn_decode_body(*refs, fox, n_steps, pps, tn, page):
    kv_refs = refs[4:4 + 2 * pps]
    k_refs, v_refs = kv_refs[:pps], kv_refs[pps:]
    if fox:
        lf_refs = refs[4 + 2 * pps:4 + 3 * pps]
        lfn_ref, o_ref, m_s, l_s, r_s, acc_s, c_s = refs[4 + 3 * pps:]
    else:
        o_ref, m_s, l_s, r_s, acc_s = refs[4 + 2 * pps:]
    q_ref, kn_ref, vn_ref = refs[1:4]
    st = pl.program_id(1)
    nrow = tn * N_HEADS
    ncol = page * N_HEADS

    qb = (q_ref[...] * QK_SCALE).astype(BF16)

    def same_head(width):
        r = lax.broadcasted_iota(jnp.int32, (nrow, width), 0) % N_HEADS
        c = lax.broadcasted_iota(jnp.int32, (nrow, width), 1) % N_HEADS
        return r == c

    if fox:
        @pl.when(st == 0)
        def _():
            c_s[...] = jnp.zeros_like(c_s)

    parts = []
    for u in range(pps):
        kf = k_refs[u][...].reshape(ncol, HEAD_DIM).astype(BF16)
        s = _dot_nt(qb, kf)
        if fox:
            c_loc = _cumsum_strided(lf_refs[u][...], N_HEADS, page) + c_s[...]
            c_s[...] = _tile_last(c_loc, N_HEADS, page)
            s = s - c_loc
        parts.append(s)
    s = jnp.concatenate(parts, axis=1)
    valid = same_head(pps * ncol)
    r_s[st] = jnp.sum(jnp.where(valid, s, 0.0), axis=1, keepdims=True)
    s = jnp.where(valid, s, NEG_INF)
    m = jnp.max(s, axis=1, keepdims=True)
    p = jnp.exp(s - m)
    m_s[st] = m
    l_s[st] = jnp.sum(p, axis=1, keepdims=True)
    vf = jnp.concatenate([v_refs[u][...].reshape(ncol, HEAD_DIM) for u in range(pps)], axis=0)
    acc_s[st] = _dot(p.astype(BF16), vf.astype(BF16))

    @pl.when(st == n_steps - 1)
    def _():
        s_new = _dot_nt(qb, kn_ref[...].astype(BF16))
        if fox:
            c_new = _cumsum_strided(lfn_ref[...], N_HEADS, LANES // N_HEADS) + c_s[:, :LANES]
            s_new = s_new - c_new[:, :nrow]
        key_t = lax.broadcasted_iota(jnp.int32, (nrow, nrow), 1) // N_HEADS
        qry_t = lax.broadcasted_iota(jnp.int32, (nrow, nrow), 0) // N_HEADS
        s_new = jnp.where(same_head(nrow) & (key_t <= qry_t), s_new, NEG_INF)
        m_new = jnp.max(s_new, axis=1, keepdims=True)
        p_new = jnp.exp(s_new - m_new)
        l_new = jnp.sum(p_new, axis=1, keepdims=True)
        acc_new = _dot(p_new.astype(BF16), vn_ref[...].astype(BF16))

        if fox:
            use = [None] * n_steps
        else:
            gates = [r_s[b] for b in range(n_steps)]
            use = []
            for b in range(n_steps):
                cnt = jnp.zeros((nrow, 1), jnp.int32)
                for b2 in range(n_steps):
                    if b2 == b:
                        continue
                    beats = (gates[b2] > gates[b]) | ((gates[b2] == gates[b]) & (b2 < b))
                    cnt = cnt + beats.astype(jnp.int32)
                use.append(cnt < MOBA_TOPK)

        m_tot = m_new
        for i in range(n_steps):
            mi = m_s[i] if use[i] is None else jnp.where(use[i], m_s[i], NEG_INF)
            m_tot = jnp.maximum(m_tot, mi)
        w_new = jnp.exp(m_new - m_tot)
        l_tot = w_new * l_new
        acc = w_new * acc_new
        for i in range(n_steps):
            w = jnp.exp(m_s[i] - m_tot)
            if use[i] is not None:
                w = jnp.where(use[i], w, 0.0)
            l_tot = l_tot + w * l_s[i]
            acc = acc + w * acc_s[i]
        o_ref[...] = acc / l_tot


def _attn_decode(q3, kn3, vn3, k_cache, v_cache, layer, page_table, lf_cache=None, lf_new=None):
    fox = lf_cache is not None
    sn, nrow, _ = q3.shape
    tn = nrow // N_HEADS
    n_pages = page_table.shape[1]
    page = k_cache.shape[2]
    pps = MOBA_BLOCK // page
    n_steps = n_pages // pps
    seq = lambda s, st, pt: (s, 0, 0)
    new_spec = pl.BlockSpec((None, nrow, HEAD_DIM), seq)

    def cache_spec(u):
        return pl.BlockSpec((None, None, page, N_HEADS, HEAD_DIM),
                            lambda s, st, pt: (layer, pt[s * n_pages + st * pps + u], 0, 0, 0))

    def lf_spec(u):
        return pl.BlockSpec((None, 1, page * N_HEADS),
                            lambda s, st, pt: (pt[s * n_pages + st * pps + u], 0, 0))

    in_specs = [new_spec, new_spec, new_spec] + [cache_spec(u) for u in range(pps)] * 2
    args = [q3, kn3, vn3] + [k_cache] * pps + [v_cache] * pps
    scratch = [pltpu.VMEM((n_steps, nrow, 1), F32), pltpu.VMEM((n_steps, nrow, 1), F32),
               pltpu.VMEM((n_steps, nrow, 1), F32), pltpu.VMEM((n_steps, nrow, HEAD_DIM), F32)]
    if fox:
        in_specs += [lf_spec(u) for u in range(pps)] + [pl.BlockSpec((None, 1, LANES), seq)]
        args += [lf_cache] * pps + [lf_new]
        scratch.append(pltpu.VMEM((1, page * N_HEADS), F32))
    grid_spec = pltpu.PrefetchScalarGridSpec(
        num_scalar_prefetch=1, grid=(sn, n_steps), in_specs=in_specs, out_specs=new_spec,
        scratch_shapes=scratch)
    return pl.pallas_call(
        functools.partial(_attn_decode_body, fox=fox, n_steps=n_steps, pps=pps, tn=tn, page=page),
        grid_spec=grid_spec, out_shape=jax.ShapeDtypeStruct(q3.shape, F32),
        compiler_params=_cparams(2),
        name="attn_decode_fox" if fox else "attn_decode_moba")(page_table.reshape(-1), *args)


def kernel(x_prompt, x_sample, state_s5_re, state_s5_im, cache_moba_k, cache_moba_v, cache_fox_k, cache_fox_v, cache_fox_logf, page_table, norm_g, final_norm_g, s5_w_in, s5_lam_re, s5_lam_im, s5_b_re, s5_b_im, s5_c_re, s5_c_im, s5_d, s5_log_dt, s5_w_glu, s5_b_glu, s5_w_out, moba_w_in, moba_w_out, fox_w_in, fox_b_f, fox_w_out):
    bn, t_len, _ = x_prompt.shape
    sn, tn, _ = x_sample.shape
    depth = norm_g.shape[0]
    n_pool, page = cache_moba_k.shape[1], cache_moba_k.shape[2]
    past_len = page_table.shape[1] * page
    nstate = S5_GROUPS * S5_STATE
    att = N_HEADS * HEAD_DIM

    tab_p = _rope_tables(jnp.arange(t_len, dtype=jnp.int32))
    pos_s = past_len + jnp.arange(tn, dtype=jnp.int32)
    tab_s = tuple(jnp.tile(z, (sn, 1)) for z in _rope_tables(pos_s))

    hp, hs = x_prompt, x_sample
    outs = {k: [] for k in ("s5r_p", "s5i_p", "s5r_s", "s5i_s", "mk_p", "mv_p", "mk_s", "mv_s",
                            "fk_p", "fv_p", "fl_p", "fk_s", "fv_s", "fl_s")}
    zero_state = jnp.zeros((bn, nstate), F32)
    for i in range(depth):
        kind, j = i % 3, i // 3
        last = i == depth - 1
        if kind == 0:
            prm = _s5_params(j, norm_g[i], final_norm_g, s5_w_in, s5_lam_re, s5_lam_im, s5_b_re, s5_b_im,
                             s5_c_re, s5_c_im, s5_d, s5_log_dt, s5_w_glu, s5_b_glu, s5_w_out)
            hp, re_p, im_p = _s5_layer(hp, zero_state, zero_state, prm, time_major=False, tt=32, final_norm=last)
            hs_t, re_s, im_s = _s5_layer(hs.transpose(1, 0, 2), state_s5_re[j].reshape(sn, nstate),
                                         state_s5_im[j].reshape(sn, nstate), prm,
                                         time_major=True, tt=tn, final_norm=last)
            hs = hs_t.transpose(1, 0, 2)
            outs["s5r_p"].append(re_p.reshape(bn, S5_GROUPS, S5_STATE))
            outs["s5i_p"].append(im_p.reshape(bn, S5_GROUPS, S5_STATE))
            outs["s5r_s"].append(re_s.reshape(sn, S5_GROUPS, S5_STATE))
            outs["s5i_s"].append(im_s.reshape(sn, S5_GROUPS, S5_STATE))
            continue

        fox = kind == 2
        w_in = (fox_w_in if fox else moba_w_in)[j]
        w_out = (fox_w_out if fox else moba_w_out)[j].astype(BF16)
        w4 = w_in[:, :4 * att].astype(BF16)
        xp2, xs2 = hp.reshape(bn * t_len, D_MODEL), hs.reshape(sn * tn, D_MODEL)
        if fox:
            wf_t = w_in[:, 4 * att:].T.astype(BF16)
            res_p = _inproj(xp2, norm_g[i], w4, wf_t=wf_t, b_f=fox_b_f[j])
            res_s = _inproj(xs2, norm_g[i], w4, wf_t=wf_t, b_f=fox_b_f[j])
        else:
            res_p = _inproj(xp2, norm_g[i], w4, tab_p)
            res_s = _inproj(xs2, norm_g[i], w4, tab_s)
        q_p, k_p, v_p, g_p = res_p[:4]
        q_s, k_s, v_s, g_s = res_s[:4]
        kc = cache_fox_k if fox else cache_moba_k
        vc = cache_fox_v if fox else cache_moba_v
        three = lambda z: z.reshape(sn, tn * N_HEADS, HEAD_DIM)
        if fox:
            lf_p, lf_s = res_p[4], res_s[4]
            ck = _cumsum(lf_p, t_len).reshape(N_HEAD_PAIRS, HEADS_PER_VREG, bn, t_len).transpose(2, 0, 3, 1)
            o_p = _attn_prompt(q_p, k_p, v_p, bn, t_len, ck)
            lf_s3 = lf_s.reshape(N_HEADS, sn, tn).transpose(1, 2, 0)
            lf_cache = cache_fox_logf[j].reshape(n_pool, 1, page * N_HEADS)
            lf_new = jnp.pad(lf_s3.reshape(sn, 1, tn * N_HEADS), ((0, 0), (0, 0), (0, LANES - tn * N_HEADS)))
            o_s = _attn_decode(three(q_s), three(k_s), three(v_s), kc, vc, j, page_table, lf_cache, lf_new)
            outs["fk_p"].append(k_p.reshape(bn, t_len, N_HEADS, HEAD_DIM))
            outs["fv_p"].append(v_p.reshape(bn, t_len, N_HEADS, HEAD_DIM))
            outs["fl_p"].append(lf_p.reshape(N_HEADS, bn, t_len).transpose(1, 2, 0))
            outs["fk_s"].append(k_s.reshape(sn, tn, N_HEADS, HEAD_DIM))
            outs["fv_s"].append(v_s.reshape(sn, tn, N_HEADS, HEAD_DIM))
            outs["fl_s"].append(lf_s3)
        else:
            o_p = _attn_prompt(q_p, k_p, v_p, bn, t_len)
            o_s = _attn_decode(three(q_s), three(k_s), three(v_s), kc, vc, j, page_table)
            outs["mk_p"].append(k_p.reshape(bn, t_len, N_HEADS, HEAD_DIM))
            outs["mv_p"].append(v_p.reshape(bn, t_len, N_HEADS, HEAD_DIM))
            outs["mk_s"].append(k_s.reshape(sn, tn, N_HEADS, HEAD_DIM))
            outs["mv_s"].append(v_s.reshape(sn, tn, N_HEADS, HEAD_DIM))
        hp = _outproj(o_p, g_p, xp2, w_out).reshape(bn, t_len, D_MODEL)
        hs = _outproj(o_s.reshape(sn * tn, D_MODEL), g_s, xs2, w_out).reshape(sn, tn, D_MODEL)

    st = lambda k: jnp.stack(outs[k])
    return (hp, hs, st("s5r_p"), st("s5i_p"), st("mk_p"), st("mv_p"), st("fk_p"), st("fv_p"), st("fl_p"),
            st("s5r_s"), st("s5i_s"), st("mk_s"), st("mv_s"), st("fk_s"), st("fv_s"), st("fl_s"))
```

```python
import functools
import math

import jax
import jax.numpy as jnp
from jax import lax
from jax.experimental import pallas as pl
from jax.experimental.pallas import tpu as pltpu

F32 = jnp.float32
BF16 = jnp.bfloat16

D_MODEL = 1024
N_HEADS = 16
HEAD_DIM = 64
ROPE_DIM = 16
ROPE_HALF = ROPE_DIM // 2
ROPE_THETA = 500000.0
MOBA_BLOCK = 256
MOBA_TOPK = 3
S5_GROUPS = 64
S5_GROUP = 16
S5_STATE = 64
EPS = 1e-6
NEG_INF = -1e30
QK_SCALE = HEAD_DIM ** -0.5

LANES = 128
SUBLANES = 8
HEADS_PER_VREG = LANES // HEAD_DIM
N_HEAD_PAIRS = N_HEADS // HEADS_PER_VREG
N_SLABS = D_MODEL // LANES
GROUPS_PER_SLAB = LANES // S5_GROUP
SLAB_STATE = GROUPS_PER_SLAB * S5_STATE
SCAN_SLABS = 2
DECODE_PAGES_PER_STEP = 4
VMEM_LIMIT = 52 * 1024 * 1024


def _cparams(n_axes):
    return pltpu.CompilerParams(dimension_semantics=("arbitrary",) * n_axes,
                                vmem_limit_bytes=VMEM_LIMIT)


def _rmsnorm(x, g):
    return x * lax.rsqrt(jnp.mean(x * x, axis=-1, keepdims=True) + EPS) * g


def _dot(a, b):
    return jnp.dot(a, b, preferred_element_type=F32)


def _dot_nt(a, b):
    return lax.dot_general(a, b, (((1,), (1,)), ((), ())), preferred_element_type=F32)


def _silu(x):
    return x * jax.nn.sigmoid(x)


def _gelu_tanh(x):
    return 0.5 * x * (1.0 + jnp.tanh(math.sqrt(2.0 / math.pi) * (x + 0.044715 * (x * x * x))))


def _cumsum_lanes(x, n):
    lane = lax.broadcasted_iota(jnp.int32, x.shape, x.ndim - 1)
    s = 1
    while s < n:
        x = x + jnp.where(lane >= s, pltpu.roll(x, s, x.ndim - 1), 0.0)
        s *= 2
    return x


def _s5_prep_body(lr_ref, li_ref, ldt_ref, br_ref, bi_ref, ar_ref, ai_ref, bbr_ref, bbi_ref):
    dt = jnp.exp(ldt_ref[...])
    lr, li = lr_ref[...], li_ref[...]
    mag = jnp.exp(lr * dt)
    ab_re, ab_im = mag * jnp.cos(li * dt), mag * jnp.sin(li * dt)
    nr, ni = ab_re - 1.0, ab_im
    den = lr * lr + li * li
    coef_re = (nr * lr + ni * li) / den
    coef_im = (ni * lr - nr * li) / den
    br, bi = br_ref[...], bi_ref[...]
    ar_ref[...] = ab_re
    ai_ref[...] = ab_im
    bbr_ref[...] = coef_re * br - coef_im * bi
    bbi_ref[...] = coef_re * bi + coef_im * br


def _s5_prep(lam_re, lam_im, log_dt, b_re, b_im):
    g, p, c = b_re.shape
    lr = lam_re.reshape(g, 1, p)
    li = lam_im.reshape(g, 1, p)
    ldt = jnp.broadcast_to(log_dt.reshape(g, 1, 1), (g, 1, p))
    br = b_re.transpose(0, 2, 1)
    bi = b_im.transpose(0, 2, 1)
    small = jax.ShapeDtypeStruct((g, 1, p), F32)
    big = jax.ShapeDtypeStruct((g, c, p), F32)
    return pl.pallas_call(_s5_prep_body, out_shape=(small, small, big, big), name="s5_prep")(
        lr, li, ldt, br, bi)


def _block_diag_slabs(m):
    g, c, p = m.shape
    m4 = m.reshape(N_SLABS, GROUPS_PER_SLAB, c, p)
    eye = jnp.eye(GROUPS_PER_SLAB, dtype=bool)[None, :, None, :, None]
    out = jnp.where(eye, m4[:, :, :, None, :], 0.0)
    return out.reshape(N_SLABS, GROUPS_PER_SLAB * c, GROUPS_PER_SLAB * p)


def _s5_layer_body(x_ref, g_ref, win_ref, bbd_ref, cre_ref, cim_ref, ar_ref, ai_ref, d_ref,
                   wglu_ref, bglu_ref, wout_ref, fg_ref, h0r_ref, h0i_ref,
                   o_ref, hr_ref, hi_ref,
                   x_s, proj_s, bu_s, hs_s, y_s, cr_s, ci_s,
                   *, time_major, final_norm):
    j = pl.program_id(1)
    nj = pl.num_programs(1)
    blk = x_ref.shape
    rows = blk[0] * blk[1]
    if time_major:
        tt, bt = blk[0], blk[1]
    else:
        bt, tt = blk[0], blk[1]
    nbs = bt // SUBLANES

    @pl.when(j == 0)
    def _():
        cr_s[...] = h0r_ref[...]
        ci_s[...] = h0i_ref[...]

    if time_major:
        x = x_ref[...].reshape(rows, D_MODEL)
    else:
        for t in range(tt):
            x_s[t * bt:(t + 1) * bt, :] = x_ref[:, t, :]
        x = x_s[...]
    a = _rmsnorm(x, g_ref[...]).astype(BF16)
    proj_s[...] = _dot(a, win_ref[...])
    ub = proj_s[:, :D_MODEL].astype(BF16)

    flat = (SUBLANES, SLAB_STATE)
    sw = 2 * SLAB_STATE
    for s0 in range(0, N_SLABS, SCAN_SLABS):
        slabs = range(s0, s0 + SCAN_SLABS)
        for u, s in enumerate(slabs):
            bu_s[:, u * sw:(u + 1) * sw] = _dot(ub[:, s * LANES:(s + 1) * LANES], bbd_ref[s])
        ar = [jnp.broadcast_to(ar_ref[s], flat) for s in slabs]
        ai = [jnp.broadcast_to(ai_ref[s], flat) for s in slabs]
        for bs in range(nbs):
            b0 = bs * SUBLANES

            def step(t, carry):
                r = pl.ds(pl.multiple_of(t * bt + b0, SUBLANES), SUBLANES)
                new = []
                for u in range(SCAN_SLABS):
                    h_re, h_im = carry[2 * u], carry[2 * u + 1]
                    re = slice(u * sw, u * sw + SLAB_STATE)
                    im = slice(u * sw + SLAB_STATE, (u + 1) * sw)
                    n_re = ar[u] * h_re - ai[u] * h_im + bu_s[r, re]
                    n_im = ar[u] * h_im + ai[u] * h_re + bu_s[r, im]
                    hs_s[r, re] = n_re
                    hs_s[r, im] = n_im
                    new += [n_re, n_im]
                return tuple(new)

            init = []
            for s in slabs:
                st0 = s * SLAB_STATE
                init += [cr_s[b0:b0 + SUBLANES, st0:st0 + SLAB_STATE], ci_s[b0:b0 + SUBLANES, st0:st0 + SLAB_STATE]]
            fin = lax.fori_loop(0, tt, step, tuple(init), unroll=2)
            for u, s in enumerate(slabs):
                st0 = s * SLAB_STATE
                cr_s[b0:b0 + SUBLANES, st0:st0 + SLAB_STATE] = fin[2 * u]
                ci_s[b0:b0 + SUBLANES, st0:st0 + SLAB_STATE] = fin[2 * u + 1]
        for u, s in enumerate(slabs):
            lo = s * LANES
            y = (_dot(hs_s[:, u * sw:u * sw + SLAB_STATE].astype(BF16), cre_ref[s])
                 - _dot(hs_s[:, u * sw + SLAB_STATE:(u + 1) * sw].astype(BF16), cim_ref[s]))
            y_s[:, lo:lo + LANES] = y + d_ref[:, lo:lo + LANES] * proj_s[:, lo:lo + LANES]

    gl = _gelu_tanh(y_s[...])
    z = gl * jax.nn.sigmoid(_dot(gl.astype(BF16), wglu_ref[...]) + bglu_ref[...])
    v = (z * _silu(proj_s[:, D_MODEL:])).astype(BF16)
    out = x + _dot(v, wout_ref[...])
    if final_norm:
        out = _rmsnorm(out, fg_ref[...])
    if time_major:
        o_ref[...] = out.reshape(blk)
    else:
        for t in range(tt):
            o_ref[:, t, :] = out[t * bt:(t + 1) * bt, :]

    @pl.when(j == nj - 1)
    def _():
        hr_ref[...] = cr_s[...]
        hi_ref[...] = ci_s[...]


def _s5_layer(x, h0_re, h0_im, prm, *, time_major, tt, final_norm):
    if time_major:
        t_len, bn, _ = x.shape
        bt = bn
        blk = (tt, bt, D_MODEL)
        xmap = lambda i, j: (j, i, 0)
    else:
        bn, t_len, _ = x.shape
        bt = SUBLANES
        blk = (bt, tt, D_MODEL)
        xmap = lambda i, j: (i, j, 0)
    rows = bt * tt
    nstate = S5_GROUPS * S5_STATE
    const2 = lambda i, j: (0, 0)
    const3 = lambda i, j: (0, 0, 0)
    in_specs = [
        pl.BlockSpec(blk, xmap),
        pl.BlockSpec((1, D_MODEL), const2),
        pl.BlockSpec((D_MODEL, 2 * D_MODEL), const2),
        pl.BlockSpec((N_SLABS, LANES, 2 * SLAB_STATE), const3),
        pl.BlockSpec((N_SLABS, SLAB_STATE, LANES), const3),
        pl.BlockSpec((N_SLABS, SLAB_STATE, LANES), const3),
        pl.BlockSpec((N_SLABS, 1, SLAB_STATE), const3),
        pl.BlockSpec((N_SLABS, 1, SLAB_STATE), const3),
        pl.BlockSpec((1, D_MODEL), const2),
        pl.BlockSpec((D_MODEL, D_MODEL), const2),
        pl.BlockSpec((1, D_MODEL), const2),
        pl.BlockSpec((D_MODEL, D_MODEL), const2),
        pl.BlockSpec((1, D_MODEL), const2),
        pl.BlockSpec((bt, nstate), lambda i, j: (i, 0)),
        pl.BlockSpec((bt, nstate), lambda i, j: (i, 0)),
    ]
    out_specs = (
        pl.BlockSpec(blk, xmap),
        pl.BlockSpec((bt, nstate), lambda i, j: (i, 0)),
        pl.BlockSpec((bt, nstate), lambda i, j: (i, 0)),
    )
    out_shape = (
        jax.ShapeDtypeStruct(x.shape, F32),
        jax.ShapeDtypeStruct((bn, nstate), F32),
        jax.ShapeDtypeStruct((bn, nstate), F32),
    )
    scratch = [
        pltpu.VMEM((rows, D_MODEL), F32),
        pltpu.VMEM((rows, 2 * D_MODEL), F32),
        pltpu.VMEM((rows, SCAN_SLABS * 2 * SLAB_STATE), F32),
        pltpu.VMEM((rows, SCAN_SLABS * 2 * SLAB_STATE), F32),
        pltpu.VMEM((rows, D_MODEL), F32),
        pltpu.VMEM((bt, nstate), F32),
        pltpu.VMEM((bt, nstate), F32),
    ]
    body = functools.partial(_s5_layer_body, time_major=time_major, final_norm=final_norm)
    return pl.pallas_call(
        body, grid=(bn // bt, t_len // tt), in_specs=in_specs, out_specs=out_specs,
        out_shape=out_shape, scratch_shapes=scratch, compiler_params=_cparams(2), name="s5_layer",
    )(x, prm["norm_g"], prm["w_in"], prm["bbd"], prm["c_re"], prm["c_im"], prm["ab_re"], prm["ab_im"],
      prm["d"], prm["w_glu"], prm["b_glu"], prm["w_out"], prm["final_g"], h0_re, h0_im)


def _s5_params(j, norm_g_i, final_norm_g, s5_w_in, s5_lam_re, s5_lam_im, s5_b_re, s5_b_im, s5_c_re,
               s5_c_im, s5_d, s5_log_dt, s5_w_glu, s5_b_glu, s5_w_out):
    ab_re, ab_im, bb_re, bb_im = _s5_prep(s5_lam_re[j], s5_lam_im[j], s5_log_dt[j], s5_b_re[j], s5_b_im[j])
    bbd = jnp.concatenate([_block_diag_slabs(bb_re), _block_diag_slabs(bb_im)], axis=-1).astype(BF16)
    c_re = _block_diag_slabs(s5_c_re[j]).transpose(0, 2, 1).astype(BF16)
    c_im = _block_diag_slabs(s5_c_im[j]).transpose(0, 2, 1).astype(BF16)
    return dict(
        norm_g=norm_g_i.reshape(1, D_MODEL), w_in=s5_w_in[j].astype(BF16), bbd=bbd, c_re=c_re, c_im=c_im,
        ab_re=ab_re.reshape(N_SLABS, 1, SLAB_STATE), ab_im=ab_im.reshape(N_SLABS, 1, SLAB_STATE),
        d=s5_d[j].reshape(1, D_MODEL), w_glu=s5_w_glu[j].astype(BF16), b_glu=s5_b_glu[j].reshape(1, D_MODEL),
        w_out=s5_w_out[j].astype(BF16), final_g=final_norm_g.reshape(1, D_MODEL))


def _rope_tables(pos):
    inv_freq = ROPE_THETA ** (-jnp.arange(ROPE_HALF, dtype=F32) / ROPE_HALF)
    ang = pos.astype(F32)[:, None] * inv_freq[None, :]
    cos, sin = jnp.cos(ang), jnp.sin(ang)
    r = pos.shape[0]
    ones = jnp.ones((r, HEAD_DIM - ROPE_DIM), F32)
    zeros = jnp.zeros((r, HEAD_DIM - ROPE_DIM), F32)
    zh = jnp.zeros((r, ROPE_HALF), F32)
    c = jnp.concatenate([cos, cos, ones], axis=1)
    sn = jnp.concatenate([-sin, zh, zeros], axis=1)
    sp = jnp.concatenate([zh, sin, zeros], axis=1)
    tile = lambda z: jnp.concatenate([z] * HEADS_PER_VREG, axis=1)
    return tile(c), tile(sn), tile(sp)


def _inproj_body(*refs, fox, kv_t):
    if fox:
        x_ref, g_ref, w_ref, wf_ref, bf_ref, q_ref, k_ref, v_ref, gate_ref, lf_ref = refs
    else:
        x_ref, g_ref, w_ref, cos_ref, sn_ref, sp_ref, q_ref, k_ref, v_ref, gate_ref = refs
    a = _rmsnorm(x_ref[...], g_ref[...]).astype(BF16)
    proj = _dot(a, w_ref[...])

    def put(dst, c, z):
        if kv_t:
            dst[c * LANES:(c + 1) * LANES, :] = z.T
        else:
            dst[:, c * LANES:(c + 1) * LANES] = z

    nchunk = D_MODEL // LANES
    chunk = lambda base, c: proj[:, base + c * LANES: base + (c + 1) * LANES]
    if fox:
        q_ref[...] = proj[:, :D_MODEL]
        for c in range(nchunk):
            put(k_ref, c, chunk(D_MODEL, c))
        z = _dot_nt(wf_ref[...], a) + bf_ref[...]
        lf_ref[...] = jnp.minimum(z, 0.0) - jnp.log1p(jnp.exp(-jnp.abs(z)))
    else:
        cos, sn, sp = cos_ref[...], sn_ref[...], sp_ref[...]
        rot = lambda z: z * cos + pltpu.roll(z, LANES - ROPE_HALF, 1) * sn + pltpu.roll(z, ROPE_HALF, 1) * sp
        for c in range(nchunk):
            q_ref[:, c * LANES:(c + 1) * LANES] = rot(chunk(0, c))
            put(k_ref, c, rot(chunk(D_MODEL, c)))
    for c in range(nchunk):
        put(v_ref, c, chunk(2 * D_MODEL, c))
    gate_ref[...] = proj[:, 3 * D_MODEL:]


def _inproj(x2d, norm_g, w4, tables=None, wf_t=None, b_f=None, *, tm=256, seq_len=None):
    n = x2d.shape[0]
    fox = wf_t is not None
    kv_t = seq_len is not None
    row = lambda i: (i, 0)
    const = lambda i: (0, 0)
    in_specs = [pl.BlockSpec((tm, D_MODEL), row), pl.BlockSpec((1, D_MODEL), const),
                pl.BlockSpec((D_MODEL, 4 * D_MODEL), const)]
    out_specs = [pl.BlockSpec((tm, D_MODEL), row)] * 4
    out_shape = [jax.ShapeDtypeStruct((n, D_MODEL), F32)] * 4
    if kv_t:
        tps = seq_len // tm
        kv_spec = pl.BlockSpec((None, D_MODEL, tm), lambda i: (i // tps, 0, i % tps))
        kv_shape = jax.ShapeDtypeStruct((n // seq_len, D_MODEL, seq_len), F32)
        out_specs[1:3] = [kv_spec, kv_spec]
        out_shape[1:3] = [kv_shape, kv_shape]
    args = [x2d, norm_g.reshape(1, D_MODEL), w4]
    if fox:
        in_specs += [pl.BlockSpec((N_HEADS, D_MODEL), const), pl.BlockSpec((N_HEADS, 1), const)]
        out_specs += [pl.BlockSpec((N_HEADS, tm), lambda i: (0, i))]
        out_shape += [jax.ShapeDtypeStruct((N_HEADS, n), F32)]
        args += [wf_t, b_f.reshape(N_HEADS, 1)]
    else:
        nrep = tables[0].shape[0] // tm
        tab = lambda i: (i % nrep, 0)
        in_specs += [pl.BlockSpec((tm, LANES), tab)] * 3
        args += list(tables)
    return pl.pallas_call(
        functools.partial(_inproj_body, fox=fox, kv_t=kv_t), grid=(n // tm,), in_specs=in_specs,
        out_specs=tuple(out_specs), out_shape=tuple(out_shape), compiler_params=_cparams(1),
        name="inproj_fox" if fox else "inproj_moba")(*args)


def _outproj_body(o_ref, gate_ref, x_ref, w_ref, y_ref):
    v = (o_ref[...] * _silu(gate_ref[...])).astype(BF16)
    y_ref[...] = x_ref[...] + _dot(v, w_ref[...])


def _outproj(o2d, gate2d, x2d, w, *, tm=512):
    n = x2d.shape[0]
    row = lambda i: (i, 0)
    spec = pl.BlockSpec((tm, D_MODEL), row)
    return pl.pallas_call(
        _outproj_body, grid=(n // tm,),
        in_specs=[spec, spec, spec, pl.BlockSpec((D_MODEL, D_MODEL), lambda i: (0, 0))],
        out_specs=spec, out_shape=jax.ShapeDtypeStruct((n, D_MODEL), F32),
        compiler_params=_cparams(1), name="outproj")(o2d, gate2d, x2d, w)


def _cumsum_body(x_ref, o_ref):
    o_ref[...] = _cumsum_lanes(x_ref[...], x_ref.shape[-1])


def _cumsum(lf_t, t_len):
    h, n = lf_t.shape
    spec = pl.BlockSpec((h, t_len), lambda b: (0, b))
    return pl.pallas_call(_cumsum_body, grid=(n // t_len,), in_specs=[spec], out_specs=spec,
                          out_shape=jax.ShapeDtypeStruct((h, n), F32),
                          compiler_params=_cparams(1), name="cumsum")(lf_t)


def _head_mask(h):
    lane = lax.broadcasted_iota(jnp.int32, (1, LANES), 1)
    return (lane // HEAD_DIM) == h


def _attn_prompt_body(*refs, fox, tq):
    if fox:
        q_ref, k_ref, v_ref, ck_ref, o_ref, kb_s, vt_s, ckb_s = refs
    else:
        q_ref, k_ref, v_ref, o_ref, kb_s, vt_s, km_s = refs
    t_len = k_ref.shape[1]
    nb = t_len // tq

    for i in range(nb):
        k = k_ref[:, i * tq:(i + 1) * tq].T
        kb_s[i * tq:(i + 1) * tq, :] = k.astype(BF16)
        vt_s[i] = v_ref[:, i * tq:(i + 1) * tq].astype(BF16)
        if not fox:
            km_s[i:i + 1, :] = jnp.mean(k, axis=0, keepdims=True)
    if fox:
        for h in range(HEADS_PER_VREG):
            ckb_s[h] = jnp.broadcast_to(ck_ref[:, h:h + 1], (t_len, LANES))

    key_id = lax.broadcasted_iota(jnp.int32, (tq, tq), 0)
    qry_id = lax.broadcasted_iota(jnp.int32, (tq, tq), 1)
    causal = key_id <= qry_id
    blk_id = lax.broadcasted_iota(jnp.int32, (nb, tq), 0)
    for qi in range(nb):
        q = q_ref[qi * tq:(qi + 1) * tq, :]
        o_parts = []
        for h in range(HEADS_PER_VREG):
            hd = slice(h * HEAD_DIM, (h + 1) * HEAD_DIM)
            qm = jnp.where(_head_mask(h), q, 0.0)
            qt = (qm * QK_SCALE).T.astype(BF16)
            sel_bias = None
            if not fox and qi > MOBA_TOPK:
                gate = lax.dot_general(km_s[...], qm, (((1,), (1,)), ((), ())),
                                       precision=lax.Precision.HIGHEST, preferred_element_type=F32)
                gate = jnp.where(blk_id < qi, gate, NEG_INF)
                cnt = jnp.zeros((nb, tq), jnp.int32)
                for i in range(qi):
                    row = gate[i:i + 1, :]
                    beats = (row > gate) | ((row == gate) & (blk_id > i))
                    cnt = cnt + beats.astype(jnp.int32)
                sel_bias = jnp.where(cnt < MOBA_TOPK, 0.0, NEG_INF)

            def scores(i):
                s = _dot(kb_s[i * tq:(i + 1) * tq, :], qt)
                if fox:
                    b = ckb_s[h, i * tq:(i + 1) * tq, :]
                    s = s - jnp.concatenate([b] * (tq // LANES), axis=1)
                return s

            s = jnp.where(causal, scores(qi), NEG_INF)
            m = jnp.max(s, axis=0, keepdims=True)
            p = jnp.exp(s - m)
            l = jnp.sum(p, axis=0, keepdims=True)
            acc = _dot(vt_s[qi, hd, :], p.astype(BF16))
            for i in range(qi):
                s = scores(i)
                if sel_bias is not None:
                    s = s + sel_bias[i:i + 1, :]
                m_new = jnp.maximum(m, jnp.max(s, axis=0, keepdims=True))
                alpha = jnp.exp(m - m_new)
                p = jnp.exp(s - m_new)
                l = alpha * l + jnp.sum(p, axis=0, keepdims=True)
                acc = alpha * acc + _dot(vt_s[i, hd, :], p.astype(BF16))
                m = m_new
            o_parts.append(acc / l)
        o_ref[qi * tq:(qi + 1) * tq, :] = jnp.concatenate(o_parts, axis=0).T


def _attn_prompt(q2d, k_t, v_t, bn, t_len, ck=None, *, tq=MOBA_BLOCK):
    fox = ck is not None
    nq = t_len // tq
    spec = pl.BlockSpec((t_len, LANES), lambda b, hp: (b, hp))
    kv_spec = pl.BlockSpec((None, LANES, t_len), lambda b, hp: (b, hp, 0))
    in_specs = [spec, kv_spec, kv_spec]
    args = [q2d, k_t, v_t]
    scratch = [pltpu.VMEM((t_len, LANES), BF16), pltpu.VMEM((nq, LANES, tq), BF16)]
    if fox:
        in_specs.append(pl.BlockSpec((None, None, t_len, HEADS_PER_VREG), lambda b, hp: (b, hp, 0, 0)))
        args.append(ck)
        scratch.append(pltpu.VMEM((HEADS_PER_VREG, t_len, LANES), F32))
    else:
        scratch.append(pltpu.VMEM((nq, LANES), F32))
    return pl.pallas_call(
        functools.partial(_attn_prompt_body, fox=fox, tq=tq), grid=(bn, N_HEAD_PAIRS),
        in_specs=in_specs, out_specs=spec, out_shape=jax.ShapeDtypeStruct(q2d.shape, F32),
        scratch_shapes=scratch, compiler_params=_cparams(2),
        name="attn_prompt_fox" if fox else "attn_prompt_moba")(*args)


def _attn_decode_body(*refs, fox, n_pages, pps, ppb, tn, page):
    k_refs, v_refs = refs[4:4 + pps], refs[4 + pps:4 + 2 * pps]
    if fox:
        lf_refs = refs[4 + 2 * pps:4 + 3 * pps]
        lfn_ref, o_ref, qbd_s, m_s, l_s, r_s, acc_s, pad_s, c_s = refs[4 + 3 * pps:]
    else:
        o_ref, qbd_s, m_s, l_s, r_s, acc_s, pad_s = refs[4 + 2 * pps:]
    q_ref, kn_ref, vn_ref = refs[1:4]
    st = pl.program_id(1)
    n_steps = n_pages // pps
    nblk = n_pages // ppb
    bps = pps // ppb
    nrow = tn * N_HEADS
    row_head = lax.broadcasted_iota(jnp.int32, (nrow, D_MODEL), 0) % N_HEADS
    lane_head = lax.broadcasted_iota(jnp.int32, (nrow, D_MODEL), 1) // HEAD_DIM
    head_mask = row_head == lane_head

    def rep_rows(z):
        return jnp.concatenate([jnp.broadcast_to(z[i:i + 1], (N_HEADS, z.shape[1])) for i in range(tn)], axis=0)

    @pl.when(st == 0)
    def _():
        qbd_s[...] = jnp.where(head_mask, rep_rows(q_ref[...]) * QK_SCALE, 0.0).astype(BF16)
        if fox:
            c_s[...] = jnp.zeros_like(c_s)

    qbd = qbd_s[...]
    for blk in range(bps):
        pages = range(blk * ppb, (blk + 1) * ppb)
        parts = []
        for u in pages:
            s = _dot(qbd, k_refs[u][...].reshape(D_MODEL, page).astype(BF16))
            if fox:
                c_loc = _cumsum_lanes(lf_refs[u][...], page) + c_s[...]
                c_s[...] = jnp.broadcast_to(c_loc[:, page - 1:page], c_s.shape)
                s = s - jnp.concatenate([c_loc] * tn, axis=0)
            parts.append(s)
        s = jnp.concatenate(parts, axis=1)
        idx = st * bps + blk
        if not fox:
            r_s[idx] = jnp.sum(s, axis=1, keepdims=True)
        m = jnp.max(s, axis=1, keepdims=True)
        p = jnp.exp(s - m)
        m_s[idx] = m
        l_s[idx] = jnp.sum(p, axis=1, keepdims=True)
        p = p.astype(BF16)
        acc = None
        for n, u in enumerate(pages):
            part = _dot_nt(p[:, n * page:(n + 1) * page], v_refs[u][...].reshape(D_MODEL, page).astype(BF16))
            acc = part if acc is None else acc + part
        acc_s[idx] = acc

    @pl.when(st == n_steps - 1)
    def _():
        pad_s[...] = jnp.zeros_like(pad_s)
        pad_s[0:tn, :] = kn_ref[...]
        s_new = _dot_nt(qbd, pad_s[...].astype(BF16))
        if fox:
            c_new = _cumsum_lanes(lfn_ref[...], page) + c_s[...]
            s_new = s_new - jnp.concatenate([c_new] * tn, axis=0)
        key_id = lax.broadcasted_iota(jnp.int32, (nrow, page), 1)
        q_id = lax.broadcasted_iota(jnp.int32, (nrow, page), 0) // N_HEADS
        s_new = jnp.where(key_id <= q_id, s_new, NEG_INF)
        pad_s[0:tn, :] = vn_ref[...]
        m_new = jnp.max(s_new, axis=1, keepdims=True)
        p_new = jnp.exp(s_new - m_new)
        l_new = jnp.sum(p_new, axis=1, keepdims=True)
        acc_new = _dot(p_new.astype(BF16), pad_s[...].astype(BF16))

        if fox:
            use = [None] * nblk
        else:
            gates = [r_s[b] for b in range(nblk)]
            use = []
            for b in range(nblk):
                cnt = jnp.zeros((nrow, 1), jnp.int32)
                for b2 in range(nblk):
                    if b2 == b:
                        continue
                    beats = (gates[b2] > gates[b]) | ((gates[b2] == gates[b]) & (b2 < b))
                    cnt = cnt + beats.astype(jnp.int32)
                use.append(cnt < MOBA_TOPK)

        m_tot = m_new
        for i in range(nblk):
            mi = m_s[i] if use[i] is None else jnp.where(use[i], m_s[i], NEG_INF)
            m_tot = jnp.maximum(m_tot, mi)
        w_new = jnp.exp(m_new - m_tot)
        l_tot = w_new * l_new
        acc = w_new * acc_new
        for i in range(nblk):
            w = jnp.exp(m_s[i] - m_tot)
            if use[i] is not None:
                w = jnp.where(use[i], w, 0.0)
            l_tot = l_tot + w * l_s[i]
            acc = acc + w * acc_s[i]
        o_full = jnp.where(head_mask, acc / l_tot, 0.0)
        for i in range(tn):
            o_ref[i:i + 1, :] = jnp.sum(o_full[i * N_HEADS:(i + 1) * N_HEADS, :], axis=0, keepdims=True)


def _attn_decode(q3, kn3, vn3, k_cache_t, v_cache_t, layer, page_table, lf_cache_t=None, lf_new=None):
    fox = lf_cache_t is not None
    sn, tn, _ = q3.shape
    n_pages = page_table.shape[1]
    page = k_cache_t.shape[-1]
    ppb = MOBA_BLOCK // page
    pps = DECODE_PAGES_PER_STEP
    assert pps % ppb == 0 and n_pages % pps == 0
    nblk = n_pages // ppb
    nrow = tn * N_HEADS
    seq = lambda s, st, pt: (s, 0, 0)
    new_spec = pl.BlockSpec((None, tn, D_MODEL), seq)

    def cache_spec(u):
        return pl.BlockSpec((None, None, N_HEADS, HEAD_DIM, page),
                            lambda s, st, pt: (layer, pt[s * n_pages + st * pps + u], 0, 0, 0))

    def lf_spec(u):
        return pl.BlockSpec((None, N_HEADS, page), lambda s, st, pt: (pt[s * n_pages + st * pps + u], 0, 0))

    in_specs = [new_spec, new_spec, new_spec] + [cache_spec(u) for u in range(pps)] * 2
    args = [q3, kn3, vn3] + [k_cache_t] * pps + [v_cache_t] * pps
    scratch = [pltpu.VMEM((nrow, D_MODEL), BF16),
               pltpu.VMEM((nblk, nrow, 1), F32), pltpu.VMEM((nblk, nrow, 1), F32),
               pltpu.VMEM((nblk, nrow, 1), F32), pltpu.VMEM((nblk, nrow, D_MODEL), F32),
               pltpu.VMEM((page, D_MODEL), F32)]
    if fox:
        in_specs += [lf_spec(u) for u in range(pps)] + [pl.BlockSpec((None, N_HEADS, page), seq)]
        args += [lf_cache_t] * pps + [lf_new]
        scratch.append(pltpu.VMEM((N_HEADS, page), F32))
    grid_spec = pltpu.PrefetchScalarGridSpec(
        num_scalar_prefetch=1, grid=(sn, n_pages // pps), in_specs=in_specs, out_specs=new_spec,
        scratch_shapes=scratch)
    return pl.pallas_call(
        functools.partial(_attn_decode_body, fox=fox, n_pages=n_pages, pps=pps, ppb=ppb, tn=tn, page=page),
        grid_spec=grid_spec, out_shape=jax.ShapeDtypeStruct(q3.shape, F32),
        compiler_params=_cparams(2),
        name="attn_decode_fox" if fox else "attn_decode_moba")(page_table.reshape(-1), *args)


def kernel(x_prompt, x_sample, state_s5_re, state_s5_im, cache_moba_k, cache_moba_v, cache_fox_k, cache_fox_v, cache_fox_logf, page_table, norm_g, final_norm_g, s5_w_in, s5_lam_re, s5_lam_im, s5_b_re, s5_b_im, s5_c_re, s5_c_im, s5_d, s5_log_dt, s5_w_glu, s5_b_glu, s5_w_out, moba_w_in, moba_w_out, fox_w_in, fox_b_f, fox_w_out):
    bn, t_len, _ = x_prompt.shape
    sn, tn, _ = x_sample.shape
    depth = norm_g.shape[0]
    n_pool, page = cache_moba_k.shape[1], cache_moba_k.shape[2]
    past_len = page_table.shape[1] * page
    nstate = S5_GROUPS * S5_STATE
    att = N_HEADS * HEAD_DIM

    tab_p = _rope_tables(jnp.arange(t_len, dtype=jnp.int32))
    pos_s = past_len + jnp.arange(tn, dtype=jnp.int32)
    tab_s = tuple(jnp.tile(z, (sn, 1)) for z in _rope_tables(pos_s))

    hp, hs = x_prompt, x_sample
    outs = {k: [] for k in ("s5r_p", "s5i_p", "s5r_s", "s5i_s", "mk_p", "mv_p", "mk_s", "mv_s",
                            "fk_p", "fv_p", "fl_p", "fk_s", "fv_s", "fl_s")}
    zero_state = jnp.zeros((bn, nstate), F32)
    for i in range(depth):
        kind, j = i % 3, i // 3
        last = i == depth - 1
        if kind == 0:
            prm = _s5_params(j, norm_g[i], final_norm_g, s5_w_in, s5_lam_re, s5_lam_im, s5_b_re, s5_b_im,
                             s5_c_re, s5_c_im, s5_d, s5_log_dt, s5_w_glu, s5_b_glu, s5_w_out)
            hp, re_p, im_p = _s5_layer(hp, zero_state, zero_state, prm, time_major=False, tt=32, final_norm=last)
            hs_t, re_s, im_s = _s5_layer(hs.transpose(1, 0, 2), state_s5_re[j].reshape(sn, nstate),
                                         state_s5_im[j].reshape(sn, nstate), prm,
                                         time_major=True, tt=tn, final_norm=last)
            hs = hs_t.transpose(1, 0, 2)
            outs["s5r_p"].append(re_p.reshape(bn, S5_GROUPS, S5_STATE))
            outs["s5i_p"].append(im_p.reshape(bn, S5_GROUPS, S5_STATE))
            outs["s5r_s"].append(re_s.reshape(sn, S5_GROUPS, S5_STATE))
            outs["s5i_s"].append(im_s.reshape(sn, S5_GROUPS, S5_STATE))
            continue

        fox = kind == 2
        w_in = (fox_w_in if fox else moba_w_in)[j]
        w_out = (fox_w_out if fox else moba_w_out)[j].astype(BF16)
        w4 = w_in[:, :4 * att].astype(BF16)
        xp2, xs2 = hp.reshape(bn * t_len, D_MODEL), hs.reshape(sn * tn, D_MODEL)
        if fox:
            wf_t = w_in[:, 4 * att:].T.astype(BF16)
            res_p = _inproj(xp2, norm_g[i], w4, wf_t=wf_t, b_f=fox_b_f[j], seq_len=t_len)
            res_s = _inproj(xs2, norm_g[i], w4, wf_t=wf_t, b_f=fox_b_f[j])
        else:
            res_p = _inproj(xp2, norm_g[i], w4, tab_p, seq_len=t_len)
            res_s = _inproj(xs2, norm_g[i], w4, tab_s)
        q_p, k_p, v_p, g_p = res_p[:4]
        q_s, k_s, v_s, g_s = res_s[:4]
        kc = (cache_fox_k if fox else cache_moba_k).transpose(0, 1, 3, 4, 2)
        vc = (cache_fox_v if fox else cache_moba_v).transpose(0, 1, 3, 4, 2)
        three = lambda z: z.reshape(sn, tn, D_MODEL)
        heads_last = lambda z: z.reshape(bn, N_HEADS, HEAD_DIM, t_len).transpose(0, 3, 1, 2)
        if fox:
            lf_p, lf_s = res_p[4], res_s[4]
            ck = _cumsum(lf_p, t_len).reshape(N_HEAD_PAIRS, HEADS_PER_VREG, bn, t_len).transpose(2, 0, 3, 1)
            o_p = _attn_prompt(q_p, k_p, v_p, bn, t_len, ck)
            lf_cache_t = cache_fox_logf[j].transpose(0, 2, 1)
            lf_new = jnp.pad(lf_s.reshape(N_HEADS, sn, tn).transpose(1, 0, 2), ((0, 0), (0, 0), (0, page - tn)))
            o_s = _attn_decode(three(q_s), three(k_s), three(v_s), kc, vc, j, page_table, lf_cache_t, lf_new)
            outs["fk_p"].append(heads_last(k_p))
            outs["fv_p"].append(heads_last(v_p))
            outs["fl_p"].append(lf_p.reshape(N_HEADS, bn, t_len).transpose(1, 2, 0))
            outs["fk_s"].append(k_s.reshape(sn, tn, N_HEADS, HEAD_DIM))
            outs["fv_s"].append(v_s.reshape(sn, tn, N_HEADS, HEAD_DIM))
            outs["fl_s"].append(lf_s.reshape(N_HEADS, sn, tn).transpose(1, 2, 0))
        else:
            o_p = _attn_prompt(q_p, k_p, v_p, bn, t_len)
            o_s = _attn_decode(three(q_s), three(k_s), three(v_s), kc, vc, j, page_table)
            outs["mk_p"].append(heads_last(k_p))
            outs["mv_p"].append(heads_last(v_p))
            outs["mk_s"].append(k_s.reshape(sn, tn, N_HEADS, HEAD_DIM))
            outs["mv_s"].append(v_s.reshape(sn, tn, N_HEADS, HEAD_DIM))
        hp = _outproj(o_p, g_p, xp2, w_out).reshape(bn, t_len, D_MODEL)
        hs = _outproj(o_s.reshape(sn * tn, D_MODEL), g_s, xs2, w_out).reshape(sn, tn, D_MODEL)

    st = lambda k: jnp.stack(outs[k])
    return (hp, hs, st("s5r_p"), st("s5i_p"), st("mk_p"), st("mv_p"), st("fk_p"), st("fv_p"), st("fl_p"),
            st("s5r_s"), st("s5i_s"), st("mk_s"), st("mv_s"), st("fk_s"), st("fv_s"), st("fl_s"))
```

```python
import functools
import math

import jax
import jax.numpy as jnp
from jax import lax
from jax.experimental import pallas as pl
from jax.experimental.pallas import tpu as pltpu

F32 = jnp.float32
BF16 = jnp.bfloat16

D_MODEL = 1024
N_HEADS = 16
HEAD_DIM = 64
ROPE_DIM = 16
ROPE_HALF = ROPE_DIM // 2
ROPE_THETA = 500000.0
MOBA_BLOCK = 256
MOBA_TOPK = 3
S5_GROUPS = 64
S5_GROUP = 16
S5_STATE = 64
EPS = 1e-6
NEG_INF = -1e30
QK_SCALE = HEAD_DIM ** -0.5

LANES = 128
SUBLANES = 8
HEADS_PER_VREG = LANES // HEAD_DIM
N_HEAD_PAIRS = N_HEADS // HEADS_PER_VREG
N_SLABS = D_MODEL // LANES
GROUPS_PER_SLAB = LANES // S5_GROUP
SLAB_STATE = GROUPS_PER_SLAB * S5_STATE
SCAN_SLABS = 2
DECODE_PAGES_PER_STEP = 8
KEY_TILES = 4
VMEM_LIMIT = 52 * 1024 * 1024


def _cparams(n_axes):
    return pltpu.CompilerParams(dimension_semantics=("arbitrary",) * n_axes,
                                vmem_limit_bytes=VMEM_LIMIT)


def _rmsnorm(x, g):
    return x * lax.rsqrt(jnp.mean(x * x, axis=-1, keepdims=True) + EPS) * g


def _dot(a, b):
    return jnp.dot(a, b, preferred_element_type=F32)


def _dot_nt(a, b):
    return lax.dot_general(a, b, (((1,), (1,)), ((), ())), preferred_element_type=F32)


def _silu(x):
    return x * jax.nn.sigmoid(x)


def _gelu_tanh(x):
    return 0.5 * x * (1.0 + jnp.tanh(math.sqrt(2.0 / math.pi) * (x + 0.044715 * (x * x * x))))


def _cumsum_lanes(x, n):
    lane = lax.broadcasted_iota(jnp.int32, x.shape, x.ndim - 1)
    s = 1
    while s < n:
        x = x + jnp.where(lane >= s, pltpu.roll(x, s, x.ndim - 1), 0.0)
        s *= 2
    return x


def _s5_prep_body(lr_ref, li_ref, ldt_ref, br_ref, bi_ref, ar_ref, ai_ref, bbr_ref, bbi_ref):
    dt = jnp.exp(ldt_ref[...])
    lr, li = lr_ref[...], li_ref[...]
    mag = jnp.exp(lr * dt)
    ab_re, ab_im = mag * jnp.cos(li * dt), mag * jnp.sin(li * dt)
    nr, ni = ab_re - 1.0, ab_im
    den = lr * lr + li * li
    coef_re = (nr * lr + ni * li) / den
    coef_im = (ni * lr - nr * li) / den
    br, bi = br_ref[...], bi_ref[...]
    ar_ref[...] = ab_re
    ai_ref[...] = ab_im
    bbr_ref[...] = coef_re * br - coef_im * bi
    bbi_ref[...] = coef_re * bi + coef_im * br


def _s5_prep(lam_re, lam_im, log_dt, b_re, b_im):
    g, p, c = b_re.shape
    lr = lam_re.reshape(g, 1, p)
    li = lam_im.reshape(g, 1, p)
    ldt = jnp.broadcast_to(log_dt.reshape(g, 1, 1), (g, 1, p))
    br = b_re.transpose(0, 2, 1)
    bi = b_im.transpose(0, 2, 1)
    small = jax.ShapeDtypeStruct((g, 1, p), F32)
    big = jax.ShapeDtypeStruct((g, c, p), F32)
    return pl.pallas_call(_s5_prep_body, out_shape=(small, small, big, big), name="s5_prep")(
        lr, li, ldt, br, bi)


def _block_diag_slabs(m):
    g, c, p = m.shape
    m4 = m.reshape(N_SLABS, GROUPS_PER_SLAB, c, p)
    eye = jnp.eye(GROUPS_PER_SLAB, dtype=bool)[None, :, None, :, None]
    out = jnp.where(eye, m4[:, :, :, None, :], 0.0)
    return out.reshape(N_SLABS, GROUPS_PER_SLAB * c, GROUPS_PER_SLAB * p)


def _s5_layer_body(x_ref, g_ref, win_ref, bbd_ref, cre_ref, cim_ref, ar_ref, ai_ref, d_ref,
                   wglu_ref, bglu_ref, wout_ref, fg_ref, h0r_ref, h0i_ref,
                   o_ref, hr_ref, hi_ref,
                   x_s, proj_s, bu_s, hs_s, y_s, cr_s, ci_s,
                   *, time_major, final_norm):
    j = pl.program_id(1)
    nj = pl.num_programs(1)
    blk = x_ref.shape
    rows = blk[0] * blk[1]
    if time_major:
        tt, bt = blk[0], blk[1]
    else:
        bt, tt = blk[0], blk[1]
    nbs = bt // SUBLANES

    @pl.when(j == 0)
    def _():
        cr_s[...] = h0r_ref[...]
        ci_s[...] = h0i_ref[...]

    if time_major:
        x = x_ref[...].reshape(rows, D_MODEL)
    else:
        for t in range(tt):
            x_s[t * bt:(t + 1) * bt, :] = x_ref[:, t, :]
        x = x_s[...]
    a = _rmsnorm(x, g_ref[...]).astype(BF16)
    proj_s[...] = _dot(a, win_ref[...])
    ub = proj_s[:, :D_MODEL].astype(BF16)

    flat = (SUBLANES, SLAB_STATE)
    sw = 2 * SLAB_STATE
    for s0 in range(0, N_SLABS, SCAN_SLABS):
        slabs = range(s0, s0 + SCAN_SLABS)
        for u, s in enumerate(slabs):
            bu_s[:, u * sw:(u + 1) * sw] = _dot(ub[:, s * LANES:(s + 1) * LANES], bbd_ref[s])
        ar = [jnp.broadcast_to(ar_ref[s], flat) for s in slabs]
        ai = [jnp.broadcast_to(ai_ref[s], flat) for s in slabs]
        for bs in range(nbs):
            b0 = bs * SUBLANES

            def step(t, carry):
                r = pl.ds(pl.multiple_of(t * bt + b0, SUBLANES), SUBLANES)
                new = []
                for u in range(SCAN_SLABS):
                    h_re, h_im = carry[2 * u], carry[2 * u + 1]
                    re = slice(u * sw, u * sw + SLAB_STATE)
                    im = slice(u * sw + SLAB_STATE, (u + 1) * sw)
                    n_re = ar[u] * h_re - ai[u] * h_im + bu_s[r, re]
                    n_im = ar[u] * h_im + ai[u] * h_re + bu_s[r, im]
                    hs_s[r, re] = n_re
                    hs_s[r, im] = n_im
                    new += [n_re, n_im]
                return tuple(new)

            init = []
            for s in slabs:
                st0 = s * SLAB_STATE
                init += [cr_s[b0:b0 + SUBLANES, st0:st0 + SLAB_STATE], ci_s[b0:b0 + SUBLANES, st0:st0 + SLAB_STATE]]
            fin = lax.fori_loop(0, tt, step, tuple(init), unroll=2)
            for u, s in enumerate(slabs):
                st0 = s * SLAB_STATE
                cr_s[b0:b0 + SUBLANES, st0:st0 + SLAB_STATE] = fin[2 * u]
                ci_s[b0:b0 + SUBLANES, st0:st0 + SLAB_STATE] = fin[2 * u + 1]
        for u, s in enumerate(slabs):
            lo = s * LANES
            y = (_dot(hs_s[:, u * sw:u * sw + SLAB_STATE].astype(BF16), cre_ref[s])
                 - _dot(hs_s[:, u * sw + SLAB_STATE:(u + 1) * sw].astype(BF16), cim_ref[s]))
            y_s[:, lo:lo + LANES] = y + d_ref[:, lo:lo + LANES] * proj_s[:, lo:lo + LANES]

    gl = _gelu_tanh(y_s[...])
    z = gl * jax.nn.sigmoid(_dot(gl.astype(BF16), wglu_ref[...]) + bglu_ref[...])
    v = (z * _silu(proj_s[:, D_MODEL:])).astype(BF16)
    out = x + _dot(v, wout_ref[...])
    if final_norm:
        out = _rmsnorm(out, fg_ref[...])
    if time_major:
        o_ref[...] = out.reshape(blk)
    else:
        for t in range(tt):
            o_ref[:, t, :] = out[t * bt:(t + 1) * bt, :]

    @pl.when(j == nj - 1)
    def _():
        hr_ref[...] = cr_s[...]
        hi_ref[...] = ci_s[...]


def _s5_layer(x, h0_re, h0_im, prm, *, time_major, tt, final_norm):
    if time_major:
        t_len, bn, _ = x.shape
        bt = bn
        blk = (tt, bt, D_MODEL)
        xmap = lambda i, j: (j, i, 0)
    else:
        bn, t_len, _ = x.shape
        bt = SUBLANES
        blk = (bt, tt, D_MODEL)
        xmap = lambda i, j: (i, j, 0)
    rows = bt * tt
    nstate = S5_GROUPS * S5_STATE
    const2 = lambda i, j: (0, 0)
    const3 = lambda i, j: (0, 0, 0)
    in_specs = [
        pl.BlockSpec(blk, xmap),
        pl.BlockSpec((1, D_MODEL), const2),
        pl.BlockSpec((D_MODEL, 2 * D_MODEL), const2),
        pl.BlockSpec((N_SLABS, LANES, 2 * SLAB_STATE), const3),
        pl.BlockSpec((N_SLABS, SLAB_STATE, LANES), const3),
        pl.BlockSpec((N_SLABS, SLAB_STATE, LANES), const3),
        pl.BlockSpec((N_SLABS, 1, SLAB_STATE), const3),
        pl.BlockSpec((N_SLABS, 1, SLAB_STATE), const3),
        pl.BlockSpec((1, D_MODEL), const2),
        pl.BlockSpec((D_MODEL, D_MODEL), const2),
        pl.BlockSpec((1, D_MODEL), const2),
        pl.BlockSpec((D_MODEL, D_MODEL), const2),
        pl.BlockSpec((1, D_MODEL), const2),
        pl.BlockSpec((bt, nstate), lambda i, j: (i, 0)),
        pl.BlockSpec((bt, nstate), lambda i, j: (i, 0)),
    ]
    out_specs = (
        pl.BlockSpec(blk, xmap),
        pl.BlockSpec((bt, nstate), lambda i, j: (i, 0)),
        pl.BlockSpec((bt, nstate), lambda i, j: (i, 0)),
    )
    out_shape = (
        jax.ShapeDtypeStruct(x.shape, F32),
        jax.ShapeDtypeStruct((bn, nstate), F32),
        jax.ShapeDtypeStruct((bn, nstate), F32),
    )
    scratch = [
        pltpu.VMEM((rows, D_MODEL), F32),
        pltpu.VMEM((rows, 2 * D_MODEL), F32),
        pltpu.VMEM((rows, SCAN_SLABS * 2 * SLAB_STATE), F32),
        pltpu.VMEM((rows, SCAN_SLABS * 2 * SLAB_STATE), F32),
        pltpu.VMEM((rows, D_MODEL), F32),
        pltpu.VMEM((bt, nstate), F32),
        pltpu.VMEM((bt, nstate), F32),
    ]
    body = functools.partial(_s5_layer_body, time_major=time_major, final_norm=final_norm)
    return pl.pallas_call(
        body, grid=(bn // bt, t_len // tt), in_specs=in_specs, out_specs=out_specs,
        out_shape=out_shape, scratch_shapes=scratch, compiler_params=_cparams(2), name="s5_layer",
    )(x, prm["norm_g"], prm["w_in"], prm["bbd"], prm["c_re"], prm["c_im"], prm["ab_re"], prm["ab_im"],
      prm["d"], prm["w_glu"], prm["b_glu"], prm["w_out"], prm["final_g"], h0_re, h0_im)


def _s5_params(j, norm_g_i, final_norm_g, s5_w_in, s5_lam_re, s5_lam_im, s5_b_re, s5_b_im, s5_c_re,
               s5_c_im, s5_d, s5_log_dt, s5_w_glu, s5_b_glu, s5_w_out):
    ab_re, ab_im, bb_re, bb_im = _s5_prep(s5_lam_re[j], s5_lam_im[j], s5_log_dt[j], s5_b_re[j], s5_b_im[j])
    bbd = jnp.concatenate([_block_diag_slabs(bb_re), _block_diag_slabs(bb_im)], axis=-1).astype(BF16)
    c_re = _block_diag_slabs(s5_c_re[j]).transpose(0, 2, 1).astype(BF16)
    c_im = _block_diag_slabs(s5_c_im[j]).transpose(0, 2, 1).astype(BF16)
    return dict(
        norm_g=norm_g_i.reshape(1, D_MODEL), w_in=s5_w_in[j].astype(BF16), bbd=bbd, c_re=c_re, c_im=c_im,
        ab_re=ab_re.reshape(N_SLABS, 1, SLAB_STATE), ab_im=ab_im.reshape(N_SLABS, 1, SLAB_STATE),
        d=s5_d[j].reshape(1, D_MODEL), w_glu=s5_w_glu[j].astype(BF16), b_glu=s5_b_glu[j].reshape(1, D_MODEL),
        w_out=s5_w_out[j].astype(BF16), final_g=final_norm_g.reshape(1, D_MODEL))


def _rope_tables(pos):
    inv_freq = ROPE_THETA ** (-jnp.arange(ROPE_HALF, dtype=F32) / ROPE_HALF)
    ang = pos.astype(F32)[:, None] * inv_freq[None, :]
    cos, sin = jnp.cos(ang), jnp.sin(ang)
    r = pos.shape[0]
    ones = jnp.ones((r, HEAD_DIM - ROPE_DIM), F32)
    zeros = jnp.zeros((r, HEAD_DIM - ROPE_DIM), F32)
    zh = jnp.zeros((r, ROPE_HALF), F32)
    c = jnp.concatenate([cos, cos, ones], axis=1)
    sn = jnp.concatenate([-sin, zh, zeros], axis=1)
    sp = jnp.concatenate([zh, sin, zeros], axis=1)
    tile = lambda z: jnp.concatenate([z] * HEADS_PER_VREG, axis=1)
    return tile(c), tile(sn), tile(sp)


def _inproj_body(*refs, fox, kv_t):
    if fox:
        x_ref, g_ref, w_ref, wf_ref, bf_ref, q_ref, k_ref, v_ref, gate_ref, lf_ref = refs
    else:
        x_ref, g_ref, w_ref, cos_ref, sn_ref, sp_ref, q_ref, k_ref, v_ref, gate_ref = refs
    a = _rmsnorm(x_ref[...], g_ref[...]).astype(BF16)
    proj = _dot(a, w_ref[...])

    def put(dst, c, z):
        if kv_t:
            dst[c * LANES:(c + 1) * LANES, :] = z.T
        else:
            dst[:, c * LANES:(c + 1) * LANES] = z

    nchunk = D_MODEL // LANES
    chunk = lambda base, c: proj[:, base + c * LANES: base + (c + 1) * LANES]
    if fox:
        q_ref[...] = proj[:, :D_MODEL]
        for c in range(nchunk):
            put(k_ref, c, chunk(D_MODEL, c))
        z = _dot_nt(wf_ref[...], a) + bf_ref[...]
        lf_ref[...] = jnp.minimum(z, 0.0) - jnp.log1p(jnp.exp(-jnp.abs(z)))
    else:
        cos, sn, sp = cos_ref[...], sn_ref[...], sp_ref[...]
        rot = lambda z: z * cos + pltpu.roll(z, LANES - ROPE_HALF, 1) * sn + pltpu.roll(z, ROPE_HALF, 1) * sp
        for c in range(nchunk):
            q_ref[:, c * LANES:(c + 1) * LANES] = rot(chunk(0, c))
            put(k_ref, c, rot(chunk(D_MODEL, c)))
    for c in range(nchunk):
        put(v_ref, c, chunk(2 * D_MODEL, c))
    gate_ref[...] = proj[:, 3 * D_MODEL:]


def _inproj(x2d, norm_g, w4, tables=None, wf_t=None, b_f=None, *, tm=256, seq_len=None):
    n = x2d.shape[0]
    fox = wf_t is not None
    kv_t = seq_len is not None
    row = lambda i: (i, 0)
    const = lambda i: (0, 0)
    in_specs = [pl.BlockSpec((tm, D_MODEL), row), pl.BlockSpec((1, D_MODEL), const),
                pl.BlockSpec((D_MODEL, 4 * D_MODEL), const)]
    out_specs = [pl.BlockSpec((tm, D_MODEL), row)] * 4
    out_shape = [jax.ShapeDtypeStruct((n, D_MODEL), F32)] * 4
    if kv_t:
        tps = seq_len // tm
        kv_spec = pl.BlockSpec((None, D_MODEL, tm), lambda i: (i // tps, 0, i % tps))
        kv_shape = jax.ShapeDtypeStruct((n // seq_len, D_MODEL, seq_len), F32)
        out_specs[1:3] = [kv_spec, kv_spec]
        out_shape[1:3] = [kv_shape, kv_shape]
    args = [x2d, norm_g.reshape(1, D_MODEL), w4]
    if fox:
        in_specs += [pl.BlockSpec((N_HEADS, D_MODEL), const), pl.BlockSpec((N_HEADS, 1), const)]
        out_specs += [pl.BlockSpec((N_HEADS, tm), lambda i: (0, i))]
        out_shape += [jax.ShapeDtypeStruct((N_HEADS, n), F32)]
        args += [wf_t, b_f.reshape(N_HEADS, 1)]
    else:
        nrep = tables[0].shape[0] // tm
        tab = lambda i: (i % nrep, 0)
        in_specs += [pl.BlockSpec((tm, LANES), tab)] * 3
        args += list(tables)
    return pl.pallas_call(
        functools.partial(_inproj_body, fox=fox, kv_t=kv_t), grid=(n // tm,), in_specs=in_specs,
        out_specs=tuple(out_specs), out_shape=tuple(out_shape), compiler_params=_cparams(1),
        name="inproj_fox" if fox else "inproj_moba")(*args)


def _outproj_body(o_ref, gate_ref, x_ref, w_ref, y_ref):
    v = (o_ref[...] * _silu(gate_ref[...])).astype(BF16)
    y_ref[...] = x_ref[...] + _dot(v, w_ref[...])


def _outproj(o2d, gate2d, x2d, w, *, tm=512):
    n = x2d.shape[0]
    row = lambda i: (i, 0)
    spec = pl.BlockSpec((tm, D_MODEL), row)
    return pl.pallas_call(
        _outproj_body, grid=(n // tm,),
        in_specs=[spec, spec, spec, pl.BlockSpec((D_MODEL, D_MODEL), lambda i: (0, 0))],
        out_specs=spec, out_shape=jax.ShapeDtypeStruct((n, D_MODEL), F32),
        compiler_params=_cparams(1), name="outproj")(o2d, gate2d, x2d, w)


def _cumsum_body(x_ref, o_ref):
    o_ref[...] = _cumsum_lanes(x_ref[...], x_ref.shape[-1])


def _cumsum(lf_t, t_len):
    h, n = lf_t.shape
    spec = pl.BlockSpec((h, t_len), lambda b: (0, b))
    return pl.pallas_call(_cumsum_body, grid=(n // t_len,), in_specs=[spec], out_specs=spec,
                          out_shape=jax.ShapeDtypeStruct((h, n), F32),
                          compiler_params=_cparams(1), name="cumsum")(lf_t)


def _head_mask(h):
    lane = lax.broadcasted_iota(jnp.int32, (1, LANES), 1)
    return (lane // HEAD_DIM) == h


def _attn_prompt_body(*refs, fox, tq):
    if fox:
        q_ref, k_ref, v_ref, ck_ref, o_ref, kb_s, vt_s, ckb_s = refs
    else:
        q_ref, k_ref, v_ref, o_ref, kb_s, vt_s, km_s = refs
    t_len = k_ref.shape[1]
    nb = t_len // tq

    for i in range(nb):
        k = k_ref[:, i * tq:(i + 1) * tq].T
        kb_s[i * tq:(i + 1) * tq, :] = k.astype(BF16)
        vt_s[:, i * tq:(i + 1) * tq] = v_ref[:, i * tq:(i + 1) * tq].astype(BF16)
        if not fox:
            km_s[i:i + 1, :] = jnp.mean(k, axis=0, keepdims=True)
    if fox:
        for h in range(HEADS_PER_VREG):
            ckb_s[h] = jnp.broadcast_to(ck_ref[:, h:h + 1], (t_len, LANES))

    key_id = lax.broadcasted_iota(jnp.int32, (tq, tq), 0)
    qry_id = lax.broadcasted_iota(jnp.int32, (tq, tq), 1)
    causal = key_id <= qry_id
    blk_id = lax.broadcasted_iota(jnp.int32, (nb, tq), 0)
    for qi in range(nb):
        q = q_ref[qi * tq:(qi + 1) * tq, :]
        o_parts = []
        for h in range(HEADS_PER_VREG):
            hd = slice(h * HEAD_DIM, (h + 1) * HEAD_DIM)
            qm = jnp.where(_head_mask(h), q, 0.0)
            qt = (qm * QK_SCALE).T.astype(BF16)
            sel_bias = None
            if not fox and qi > MOBA_TOPK:
                gate = lax.dot_general(km_s[...], qm, (((1,), (1,)), ((), ())),
                                       precision=lax.Precision.HIGHEST, preferred_element_type=F32)
                gate = jnp.where(blk_id < qi, gate, NEG_INF)
                cnt = jnp.zeros((nb, tq), jnp.int32)
                for i in range(qi):
                    row = gate[i:i + 1, :]
                    beats = (row > gate) | ((row == gate) & (blk_id > i))
                    cnt = cnt + beats.astype(jnp.int32)
                sel_bias = jnp.where(cnt < MOBA_TOPK, 0.0, NEG_INF)

            def scores(i0, n):
                ks = slice(i0 * tq, (i0 + n) * tq)
                s = _dot(kb_s[ks, :], qt)
                if fox:
                    s = s - jnp.concatenate([ckb_s[h, ks, :]] * (tq // LANES), axis=1)
                elif sel_bias is not None and i0 < qi:
                    s = s + jnp.concatenate(
                        [jnp.broadcast_to(sel_bias[i:i + 1, :], (tq, tq)) for i in range(i0, i0 + n)], axis=0)
                return s, ks

            s, ks = scores(qi, 1)
            s = jnp.where(causal, s, NEG_INF)
            m = jnp.max(s, axis=0, keepdims=True)
            p = jnp.exp(s - m)
            l = jnp.sum(p, axis=0, keepdims=True)
            acc = _dot(vt_s[hd, ks], p.astype(BF16))
            for i0 in range(0, qi, KEY_TILES):
                s, ks = scores(i0, min(KEY_TILES, qi - i0))
                m_new = jnp.maximum(m, jnp.max(s, axis=0, keepdims=True))
                alpha = jnp.exp(m - m_new)
                p = jnp.exp(s - m_new)
                l = alpha * l + jnp.sum(p, axis=0, keepdims=True)
                acc = alpha * acc + _dot(vt_s[hd, ks], p.astype(BF16))
                m = m_new
            o_parts.append(acc / l)
        o_ref[qi * tq:(qi + 1) * tq, :] = jnp.concatenate(o_parts, axis=0).T


def _attn_prompt(q2d, k_t, v_t, bn, t_len, ck=None, *, tq=MOBA_BLOCK):
    fox = ck is not None
    nq = t_len // tq
    spec = pl.BlockSpec((t_len, LANES), lambda b, hp: (b, hp))
    kv_spec = pl.BlockSpec((None, LANES, t_len), lambda b, hp: (b, hp, 0))
    in_specs = [spec, kv_spec, kv_spec]
    args = [q2d, k_t, v_t]
    scratch = [pltpu.VMEM((t_len, LANES), BF16), pltpu.VMEM((LANES, t_len), BF16)]
    if fox:
        in_specs.append(pl.BlockSpec((None, None, t_len, HEADS_PER_VREG), lambda b, hp: (b, hp, 0, 0)))
        args.append(ck)
        scratch.append(pltpu.VMEM((HEADS_PER_VREG, t_len, LANES), F32))
    else:
        scratch.append(pltpu.VMEM((nq, LANES), F32))
    return pl.pallas_call(
        functools.partial(_attn_prompt_body, fox=fox, tq=tq), grid=(bn, N_HEAD_PAIRS),
        in_specs=in_specs, out_specs=spec, out_shape=jax.ShapeDtypeStruct(q2d.shape, F32),
        scratch_shapes=scratch, compiler_params=_cparams(2),
        name="attn_prompt_fox" if fox else "attn_prompt_moba")(*args)


def _attn_decode_body(*refs, fox, n_pages, pps, ppb, tn, page):
    k_refs, v_refs = refs[4:4 + pps], refs[4 + pps:4 + 2 * pps]
    if fox:
        lf_refs = refs[4 + 2 * pps:4 + 3 * pps]
        lfn_ref, o_ref, qbd_s, m_s, l_s, r_s, acc_s, pad_s, c_s = refs[4 + 3 * pps:]
    else:
        o_ref, qbd_s, m_s, l_s, r_s, acc_s, pad_s = refs[4 + 2 * pps:]
    q_ref, kn_ref, vn_ref = refs[1:4]
    st = pl.program_id(1)
    n_steps = n_pages // pps
    nblk = n_pages // ppb
    bps = pps // ppb
    nrow = tn * N_HEADS
    row_head = lax.broadcasted_iota(jnp.int32, (nrow, D_MODEL), 0) % N_HEADS
    lane_head = lax.broadcasted_iota(jnp.int32, (nrow, D_MODEL), 1) // HEAD_DIM
    head_mask = row_head == lane_head

    def rep_rows(z):
        return jnp.concatenate([jnp.broadcast_to(z[i:i + 1], (N_HEADS, z.shape[1])) for i in range(tn)], axis=0)

    @pl.when(st == 0)
    def _():
        qbd_s[...] = jnp.where(head_mask, rep_rows(q_ref[...]) * QK_SCALE, 0.0).astype(BF16)
        if fox:
            c_s[...] = jnp.zeros_like(c_s)

    qbd = qbd_s[...]
    for blk in range(bps):
        pages = range(blk * ppb, (blk + 1) * ppb)
        parts = []
        for u in pages:
            s = _dot(qbd, k_refs[u][...].reshape(D_MODEL, page).astype(BF16))
            if fox:
                c_loc = _cumsum_lanes(lf_refs[u][...], page) + c_s[...]
                c_s[...] = jnp.broadcast_to(c_loc[:, page - 1:page], c_s.shape)
                s = s - jnp.concatenate([c_loc] * tn, axis=0)
            parts.append(s)
        s = jnp.concatenate(parts, axis=1)
        idx = st * bps + blk
        if not fox:
            r_s[idx] = jnp.sum(s, axis=1, keepdims=True)
        m = jnp.max(s, axis=1, keepdims=True)
        p = jnp.exp(s - m)
        m_s[idx] = m
        l_s[idx] = jnp.sum(p, axis=1, keepdims=True)
        p = p.astype(BF16)
        acc = None
        for n, u in enumerate(pages):
            part = _dot_nt(p[:, n * page:(n + 1) * page], v_refs[u][...].reshape(D_MODEL, page).astype(BF16))
            acc = part if acc is None else acc + part
        acc_s[idx] = acc

    @pl.when(st == n_steps - 1)
    def _():
        pad_s[...] = jnp.zeros_like(pad_s)
        pad_s[0:tn, :] = kn_ref[...]
        s_new = _dot_nt(qbd, pad_s[...].astype(BF16))
        if fox:
            c_new = _cumsum_lanes(lfn_ref[...], page) + c_s[...]
            s_new = s_new - jnp.concatenate([c_new] * tn, axis=0)
        key_id = lax.broadcasted_iota(jnp.int32, (nrow, page), 1)
        q_id = lax.broadcasted_iota(jnp.int32, (nrow, page), 0) // N_HEADS
        s_new = jnp.where(key_id <= q_id, s_new, NEG_INF)
        pad_s[0:tn, :] = vn_ref[...]
        m_new = jnp.max(s_new, axis=1, keepdims=True)
        p_new = jnp.exp(s_new - m_new)
        l_new = jnp.sum(p_new, axis=1, keepdims=True)
        acc_new = _dot(p_new.astype(BF16), pad_s[...].astype(BF16))

        if fox:
            use = [None] * nblk
        else:
            gates = [r_s[b] for b in range(nblk)]
            use = []
            for b in range(nblk):
                cnt = jnp.zeros((nrow, 1), jnp.int32)
                for b2 in range(nblk):
                    if b2 == b:
                        continue
                    beats = (gates[b2] > gates[b]) | ((gates[b2] == gates[b]) & (b2 < b))
                    cnt = cnt + beats.astype(jnp.int32)
                use.append(cnt < MOBA_TOPK)

        m_tot = m_new
        for i in range(nblk):
            mi = m_s[i] if use[i] is None else jnp.where(use[i], m_s[i], NEG_INF)
            m_tot = jnp.maximum(m_tot, mi)
        w_new = jnp.exp(m_new - m_tot)
        l_tot = w_new * l_new
        acc = w_new * acc_new
        for i in range(nblk):
            w = jnp.exp(m_s[i] - m_tot)
            if use[i] is not None:
                w = jnp.where(use[i], w, 0.0)
            l_tot = l_tot + w * l_s[i]
            acc = acc + w * acc_s[i]
        o_full = jnp.where(head_mask, acc / l_tot, 0.0)
        for i in range(tn):
            o_ref[i:i + 1, :] = jnp.sum(o_full[i * N_HEADS:(i + 1) * N_HEADS, :], axis=0, keepdims=True)


def _attn_decode(q3, kn3, vn3, k_cache_t, v_cache_t, layer, page_table, lf_cache_t=None, lf_new=None):
    fox = lf_cache_t is not None
    sn, tn, _ = q3.shape
    n_pages = page_table.shape[1]
    page = k_cache_t.shape[-1]
    ppb = MOBA_BLOCK // page
    pps = DECODE_PAGES_PER_STEP
    assert pps % ppb == 0 and n_pages % pps == 0
    nblk = n_pages // ppb
    nrow = tn * N_HEADS
    seq = lambda s, st, pt: (s, 0, 0)
    new_spec = pl.BlockSpec((None, tn, D_MODEL), seq)

    def cache_spec(u):
        return pl.BlockSpec((None, None, N_HEADS, HEAD_DIM, page),
                            lambda s, st, pt: (layer, pt[s * n_pages + st * pps + u], 0, 0, 0))

    def lf_spec(u):
        return pl.BlockSpec((None, N_HEADS, page), lambda s, st, pt: (pt[s * n_pages + st * pps + u], 0, 0))

    in_specs = [new_spec, new_spec, new_spec] + [cache_spec(u) for u in range(pps)] * 2
    args = [q3, kn3, vn3] + [k_cache_t] * pps + [v_cache_t] * pps
    scratch = [pltpu.VMEM((nrow, D_MODEL), BF16),
               pltpu.VMEM((nblk, nrow, 1), F32), pltpu.VMEM((nblk, nrow, 1), F32),
               pltpu.VMEM((nblk, nrow, 1), F32), pltpu.VMEM((nblk, nrow, D_MODEL), F32),
               pltpu.VMEM((page, D_MODEL), F32)]
    if fox:
        in_specs += [lf_spec(u) for u in range(pps)] + [pl.BlockSpec((None, N_HEADS, page), seq)]
        args += [lf_cache_t] * pps + [lf_new]
        scratch.append(pltpu.VMEM((N_HEADS, page), F32))
    grid_spec = pltpu.PrefetchScalarGridSpec(
        num_scalar_prefetch=1, grid=(sn, n_pages // pps), in_specs=in_specs, out_specs=new_spec,
        scratch_shapes=scratch)
    return pl.pallas_call(
        functools.partial(_attn_decode_body, fox=fox, n_pages=n_pages, pps=pps, ppb=ppb, tn=tn, page=page),
        grid_spec=grid_spec, out_shape=jax.ShapeDtypeStruct(q3.shape, F32),
        compiler_params=_cparams(2),
        name="attn_decode_fox" if fox else "attn_decode_moba")(page_table.reshape(-1), *args)


def kernel(x_prompt, x_sample, state_s5_re, state_s5_im, cache_moba_k, cache_moba_v, cache_fox_k, cache_fox_v, cache_fox_logf, page_table, norm_g, final_norm_g, s5_w_in, s5_lam_re, s5_lam_im, s5_b_re, s5_b_im, s5_c_re, s5_c_im, s5_d, s5_log_dt, s5_w_glu, s5_b_glu, s5_w_out, moba_w_in, moba_w_out, fox_w_in, fox_b_f, fox_w_out):
    bn, t_len, _ = x_prompt.shape
    sn, tn, _ = x_sample.shape
    depth = norm_g.shape[0]
    n_pool, page = cache_moba_k.shape[1], cache_moba_k.shape[2]
    past_len = page_table.shape[1] * page
    nstate = S5_GROUPS * S5_STATE
    att = N_HEADS * HEAD_DIM

    tab_p = _rope_tables(jnp.arange(t_len, dtype=jnp.int32))
    pos_s = past_len + jnp.arange(tn, dtype=jnp.int32)
    tab_s = tuple(jnp.tile(z, (sn, 1)) for z in _rope_tables(pos_s))

    hp, hs = x_prompt, x_sample
    outs = {k: [] for k in ("s5r_p", "s5i_p", "s5r_s", "s5i_s", "mk_p", "mv_p", "mk_s", "mv_s",
                            "fk_p", "fv_p", "fl_p", "fk_s", "fv_s", "fl_s")}
    zero_state = jnp.zeros((bn, nstate), F32)
    for i in range(depth):
        kind, j = i % 3, i // 3
        last = i == depth - 1
        if kind == 0:
            prm = _s5_params(j, norm_g[i], final_norm_g, s5_w_in, s5_lam_re, s5_lam_im, s5_b_re, s5_b_im,
                             s5_c_re, s5_c_im, s5_d, s5_log_dt, s5_w_glu, s5_b_glu, s5_w_out)
            hp, re_p, im_p = _s5_layer(hp, zero_state, zero_state, prm, time_major=False, tt=32, final_norm=last)
            hs_t, re_s, im_s = _s5_layer(hs.transpose(1, 0, 2), state_s5_re[j].reshape(sn, nstate),
                                         state_s5_im[j].reshape(sn, nstate), prm,
                                         time_major=True, tt=tn, final_norm=last)
            hs = hs_t.transpose(1, 0, 2)
            outs["s5r_p"].append(re_p.reshape(bn, S5_GROUPS, S5_STATE))
            outs["s5i_p"].append(im_p.reshape(bn, S5_GROUPS, S5_STATE))
            outs["s5r_s"].append(re_s.reshape(sn, S5_GROUPS, S5_STATE))
            outs["s5i_s"].append(im_s.reshape(sn, S5_GROUPS, S5_STATE))
            continue

        fox = kind == 2
        w_in = (fox_w_in if fox else moba_w_in)[j]
        w_out = (fox_w_out if fox else moba_w_out)[j].astype(BF16)
        w4 = w_in[:, :4 * att].astype(BF16)
        xp2, xs2 = hp.reshape(bn * t_len, D_MODEL), hs.reshape(sn * tn, D_MODEL)
        if fox:
            wf_t = w_in[:, 4 * att:].T.astype(BF16)
            res_p = _inproj(xp2, norm_g[i], w4, wf_t=wf_t, b_f=fox_b_f[j], seq_len=t_len)
            res_s = _inproj(xs2, norm_g[i], w4, wf_t=wf_t, b_f=fox_b_f[j])
        else:
            res_p = _inproj(xp2, norm_g[i], w4, tab_p, seq_len=t_len)
            res_s = _inproj(xs2, norm_g[i], w4, tab_s)
        q_p, k_p, v_p, g_p = res_p[:4]
        q_s, k_s, v_s, g_s = res_s[:4]
        kc = (cache_fox_k if fox else cache_moba_k).transpose(0, 1, 3, 4, 2)
        vc = (cache_fox_v if fox else cache_moba_v).transpose(0, 1, 3, 4, 2)
        three = lambda z: z.reshape(sn, tn, D_MODEL)
        heads_last = lambda z: z.reshape(bn, N_HEADS, HEAD_DIM, t_len).transpose(0, 3, 1, 2)
        if fox:
            lf_p, lf_s = res_p[4], res_s[4]
            ck = _cumsum(lf_p, t_len).reshape(N_HEAD_PAIRS, HEADS_PER_VREG, bn, t_len).transpose(2, 0, 3, 1)
            o_p = _attn_prompt(q_p, k_p, v_p, bn, t_len, ck)
            lf_cache_t = cache_fox_logf[j].transpose(0, 2, 1)
            lf_new = jnp.pad(lf_s.reshape(N_HEADS, sn, tn).transpose(1, 0, 2), ((0, 0), (0, 0), (0, page - tn)))
            o_s = _attn_decode(three(q_s), three(k_s), three(v_s), kc, vc, j, page_table, lf_cache_t, lf_new)
            outs["fk_p"].append(heads_last(k_p))
            outs["fv_p"].append(heads_last(v_p))
            outs["fl_p"].append(lf_p.reshape(N_HEADS, bn, t_len).transpose(1, 2, 0))
            outs["fk_s"].append(k_s.reshape(sn, tn, N_HEADS, HEAD_DIM))
            outs["fv_s"].append(v_s.reshape(sn, tn, N_HEADS, HEAD_DIM))
            outs["fl_s"].append(lf_s.reshape(N_HEADS, sn, tn).transpose(1, 2, 0))
        else:
            o_p = _attn_prompt(q_p, k_p, v_p, bn, t_len)
            o_s = _attn_decode(three(q_s), three(k_s), three(v_s), kc, vc, j, page_table)
            outs["mk_p"].append(heads_last(k_p))
            outs["mv_p"].append(heads_last(v_p))
            outs["mk_s"].append(k_s.reshape(sn, tn, N_HEADS, HEAD_DIM))
            outs["mv_s"].append(v_s.reshape(sn, tn, N_HEADS, HEAD_DIM))
        hp = _outproj(o_p, g_p, xp2, w_out).reshape(bn, t_len, D_MODEL)
        hs = _outproj(o_s.reshape(sn * tn, D_MODEL), g_s, xs2, w_out).reshape(sn, tn, D_MODEL)

    st = lambda k: jnp.stack(outs[k])
    return (hp, hs, st("s5r_p"), st("s5i_p"), st("mk_p"), st("mv_p"), st("fk_p"), st("fv_p"), st("fl_p"),
            st("s5r_s"), st("s5i_s"), st("mk_s"), st("mv_s"), st("fk_s"), st("fv_s"), st("fl_s"))
```

```python
import functools
import math

import jax
import jax.numpy as jnp
from jax import lax
from jax.experimental import pallas as pl
from jax.experimental.pallas import tpu as pltpu

F32 = jnp.float32
BF16 = jnp.bfloat16

D_MODEL = 1024
N_HEADS = 16
HEAD_DIM = 64
ROPE_DIM = 16
ROPE_HALF = ROPE_DIM // 2
ROPE_THETA = 500000.0
MOBA_BLOCK = 256
MOBA_TOPK = 3
S5_GROUPS = 64
S5_GROUP = 16
S5_STATE = 64
EPS = 1e-6
NEG_INF = -1e30
QK_SCALE = HEAD_DIM ** -0.5

LANES = 128
SUBLANES = 8
HEADS_PER_VREG = LANES // HEAD_DIM
N_HEAD_PAIRS = N_HEADS // HEADS_PER_VREG
N_SLABS = D_MODEL // LANES
GROUPS_PER_SLAB = LANES // S5_GROUP
SLAB_STATE = GROUPS_PER_SLAB * S5_STATE
SCAN_SLABS = 2
DECODE_PAGES_PER_STEP = 16
KEY_TILES = 4
VMEM_LIMIT = 52 * 1024 * 1024


def _cparams(n_axes):
    return pltpu.CompilerParams(dimension_semantics=("arbitrary",) * n_axes,
                                vmem_limit_bytes=VMEM_LIMIT)


def _rmsnorm(x, g):
    return x * lax.rsqrt(jnp.mean(x * x, axis=-1, keepdims=True) + EPS) * g


def _dot(a, b):
    return jnp.dot(a, b, preferred_element_type=F32)


def _dot_nt(a, b):
    return lax.dot_general(a, b, (((1,), (1,)), ((), ())), preferred_element_type=F32)


def _silu(x):
    return x * jax.nn.sigmoid(x)


def _gelu_tanh(x):
    return 0.5 * x * (1.0 + jnp.tanh(math.sqrt(2.0 / math.pi) * (x + 0.044715 * (x * x * x))))


def _cumsum_lanes(x, n):
    lane = lax.broadcasted_iota(jnp.int32, x.shape, x.ndim - 1)
    s = 1
    while s < n:
        x = x + jnp.where(lane >= s, pltpu.roll(x, s, x.ndim - 1), 0.0)
        s *= 2
    return x


def _s5_prep_body(lr_ref, li_ref, ldt_ref, br_ref, bi_ref, ar_ref, ai_ref, bbr_ref, bbi_ref):
    dt = jnp.exp(ldt_ref[...])
    lr, li = lr_ref[...], li_ref[...]
    mag = jnp.exp(lr * dt)
    ab_re, ab_im = mag * jnp.cos(li * dt), mag * jnp.sin(li * dt)
    nr, ni = ab_re - 1.0, ab_im
    den = lr * lr + li * li
    coef_re = (nr * lr + ni * li) / den
    coef_im = (ni * lr - nr * li) / den
    br, bi = br_ref[...], bi_ref[...]
    ar_ref[...] = ab_re
    ai_ref[...] = ab_im
    bbr_ref[...] = coef_re * br - coef_im * bi
    bbi_ref[...] = coef_re * bi + coef_im * br


def _s5_prep(lam_re, lam_im, log_dt, b_re, b_im):
    g, p, c = b_re.shape
    lr = lam_re.reshape(g, 1, p)
    li = lam_im.reshape(g, 1, p)
    ldt = jnp.broadcast_to(log_dt.reshape(g, 1, 1), (g, 1, p))
    br = b_re.transpose(0, 2, 1)
    bi = b_im.transpose(0, 2, 1)
    small = jax.ShapeDtypeStruct((g, 1, p), F32)
    big = jax.ShapeDtypeStruct((g, c, p), F32)
    return pl.pallas_call(_s5_prep_body, out_shape=(small, small, big, big), name="s5_prep")(
        lr, li, ldt, br, bi)


def _block_diag_slabs(m):
    g, c, p = m.shape
    m4 = m.reshape(N_SLABS, GROUPS_PER_SLAB, c, p)
    eye = jnp.eye(GROUPS_PER_SLAB, dtype=bool)[None, :, None, :, None]
    out = jnp.where(eye, m4[:, :, :, None, :], 0.0)
    return out.reshape(N_SLABS, GROUPS_PER_SLAB * c, GROUPS_PER_SLAB * p)


def _s5_layer_body(x_ref, g_ref, win_ref, bbd_ref, cre_ref, cim_ref, ar_ref, ai_ref, d_ref,
                   wglu_ref, bglu_ref, wout_ref, fg_ref, h0r_ref, h0i_ref,
                   o_ref, hr_ref, hi_ref,
                   x_s, proj_s, bu_s, hs_s, y_s, cr_s, ci_s,
                   *, time_major, final_norm):
    j = pl.program_id(1)
    nj = pl.num_programs(1)
    blk = x_ref.shape
    rows = blk[0] * blk[1]
    if time_major:
        tt, bt = blk[0], blk[1]
    else:
        bt, tt = blk[0], blk[1]
    nbs = bt // SUBLANES

    @pl.when(j == 0)
    def _():
        cr_s[...] = h0r_ref[...]
        ci_s[...] = h0i_ref[...]

    if time_major:
        x = x_ref[...].reshape(rows, D_MODEL)
    else:
        for t in range(tt):
            x_s[t * bt:(t + 1) * bt, :] = x_ref[:, t, :]
        x = x_s[...]
    a = _rmsnorm(x, g_ref[...]).astype(BF16)
    proj_s[...] = _dot(a, win_ref[...])
    ub = proj_s[:, :D_MODEL].astype(BF16)

    flat = (SUBLANES, SLAB_STATE)
    sw = 2 * SLAB_STATE
    for s0 in range(0, N_SLABS, SCAN_SLABS):
        slabs = range(s0, s0 + SCAN_SLABS)
        for u, s in enumerate(slabs):
            bu_s[:, u * sw:(u + 1) * sw] = _dot(ub[:, s * LANES:(s + 1) * LANES], bbd_ref[s])
        ar = [jnp.broadcast_to(ar_ref[s], flat) for s in slabs]
        ai = [jnp.broadcast_to(ai_ref[s], flat) for s in slabs]
        for bs in range(nbs):
            b0 = bs * SUBLANES

            def step(t, carry):
                r = pl.ds(pl.multiple_of(t * bt + b0, SUBLANES), SUBLANES)
                new = []
                for u in range(SCAN_SLABS):
                    h_re, h_im = carry[2 * u], carry[2 * u + 1]
                    re = slice(u * sw, u * sw + SLAB_STATE)
                    im = slice(u * sw + SLAB_STATE, (u + 1) * sw)
                    n_re = ar[u] * h_re - ai[u] * h_im + bu_s[r, re]
                    n_im = ar[u] * h_im + ai[u] * h_re + bu_s[r, im]
                    hs_s[r, re] = n_re
                    hs_s[r, im] = n_im
                    new += [n_re, n_im]
                return tuple(new)

            init = []
            for s in slabs:
                st0 = s * SLAB_STATE
                init += [cr_s[b0:b0 + SUBLANES, st0:st0 + SLAB_STATE], ci_s[b0:b0 + SUBLANES, st0:st0 + SLAB_STATE]]
            fin = lax.fori_loop(0, tt, step, tuple(init), unroll=2)
            for u, s in enumerate(slabs):
                st0 = s * SLAB_STATE
                cr_s[b0:b0 + SUBLANES, st0:st0 + SLAB_STATE] = fin[2 * u]
                ci_s[b0:b0 + SUBLANES, st0:st0 + SLAB_STATE] = fin[2 * u + 1]
        for u, s in enumerate(slabs):
            lo = s * LANES
            y = (_dot(hs_s[:, u * sw:u * sw + SLAB_STATE].astype(BF16), cre_ref[s])
                 - _dot(hs_s[:, u * sw + SLAB_STATE:(u + 1) * sw].astype(BF16), cim_ref[s]))
            y_s[:, lo:lo + LANES] = y + d_ref[:, lo:lo + LANES] * proj_s[:, lo:lo + LANES]

    gl = _gelu_tanh(y_s[...])
    z = gl * jax.nn.sigmoid(_dot(gl.astype(BF16), wglu_ref[...]) + bglu_ref[...])
    v = (z * _silu(proj_s[:, D_MODEL:])).astype(BF16)
    out = x + _dot(v, wout_ref[...])
    if final_norm:
        out = _rmsnorm(out, fg_ref[...])
    if time_major:
        o_ref[...] = out.reshape(blk)
    else:
        for t in range(tt):
            o_ref[:, t, :] = out[t * bt:(t + 1) * bt, :]

    @pl.when(j == nj - 1)
    def _():
        hr_ref[...] = cr_s[...]
        hi_ref[...] = ci_s[...]


def _s5_layer(x, h0_re, h0_im, prm, *, time_major, tt, final_norm):
    if time_major:
        t_len, bn, _ = x.shape
        bt = bn
        blk = (tt, bt, D_MODEL)
        xmap = lambda i, j: (j, i, 0)
    else:
        bn, t_len, _ = x.shape
        bt = SUBLANES
        blk = (bt, tt, D_MODEL)
        xmap = lambda i, j: (i, j, 0)
    rows = bt * tt
    nstate = S5_GROUPS * S5_STATE
    const2 = lambda i, j: (0, 0)
    const3 = lambda i, j: (0, 0, 0)
    in_specs = [
        pl.BlockSpec(blk, xmap),
        pl.BlockSpec((1, D_MODEL), const2),
        pl.BlockSpec((D_MODEL, 2 * D_MODEL), const2),
        pl.BlockSpec((N_SLABS, LANES, 2 * SLAB_STATE), const3),
        pl.BlockSpec((N_SLABS, SLAB_STATE, LANES), const3),
        pl.BlockSpec((N_SLABS, SLAB_STATE, LANES), const3),
        pl.BlockSpec((N_SLABS, 1, SLAB_STATE), const3),
        pl.BlockSpec((N_SLABS, 1, SLAB_STATE), const3),
        pl.BlockSpec((1, D_MODEL), const2),
        pl.BlockSpec((D_MODEL, D_MODEL), const2),
        pl.BlockSpec((1, D_MODEL), const2),
        pl.BlockSpec((D_MODEL, D_MODEL), const2),
        pl.BlockSpec((1, D_MODEL), const2),
        pl.BlockSpec((bt, nstate), lambda i, j: (i, 0)),
        pl.BlockSpec((bt, nstate), lambda i, j: (i, 0)),
    ]
    out_specs = (
        pl.BlockSpec(blk, xmap),
        pl.BlockSpec((bt, nstate), lambda i, j: (i, 0)),
        pl.BlockSpec((bt, nstate), lambda i, j: (i, 0)),
    )
    out_shape = (
        jax.ShapeDtypeStruct(x.shape, F32),
        jax.ShapeDtypeStruct((bn, nstate), F32),
        jax.ShapeDtypeStruct((bn, nstate), F32),
    )
    scratch = [
        pltpu.VMEM((rows, D_MODEL), F32),
        pltpu.VMEM((rows, 2 * D_MODEL), F32),
        pltpu.VMEM((rows, SCAN_SLABS * 2 * SLAB_STATE), F32),
        pltpu.VMEM((rows, SCAN_SLABS * 2 * SLAB_STATE), F32),
        pltpu.VMEM((rows, D_MODEL), F32),
        pltpu.VMEM((bt, nstate), F32),
        pltpu.VMEM((bt, nstate), F32),
    ]
    body = functools.partial(_s5_layer_body, time_major=time_major, final_norm=final_norm)
    return pl.pallas_call(
        body, grid=(bn // bt, t_len // tt), in_specs=in_specs, out_specs=out_specs,
        out_shape=out_shape, scratch_shapes=scratch, compiler_params=_cparams(2), name="s5_layer",
    )(x, prm["norm_g"], prm["w_in"], prm["bbd"], prm["c_re"], prm["c_im"], prm["ab_re"], prm["ab_im"],
      prm["d"], prm["w_glu"], prm["b_glu"], prm["w_out"], prm["final_g"], h0_re, h0_im)


def _s5_params(j, norm_g_i, final_norm_g, s5_w_in, s5_lam_re, s5_lam_im, s5_b_re, s5_b_im, s5_c_re,
               s5_c_im, s5_d, s5_log_dt, s5_w_glu, s5_b_glu, s5_w_out):
    ab_re, ab_im, bb_re, bb_im = _s5_prep(s5_lam_re[j], s5_lam_im[j], s5_log_dt[j], s5_b_re[j], s5_b_im[j])
    bbd = jnp.concatenate([_block_diag_slabs(bb_re), _block_diag_slabs(bb_im)], axis=-1).astype(BF16)
    c_re = _block_diag_slabs(s5_c_re[j]).transpose(0, 2, 1).astype(BF16)
    c_im = _block_diag_slabs(s5_c_im[j]).transpose(0, 2, 1).astype(BF16)
    return dict(
        norm_g=norm_g_i.reshape(1, D_MODEL), w_in=s5_w_in[j].astype(BF16), bbd=bbd, c_re=c_re, c_im=c_im,
        ab_re=ab_re.reshape(N_SLABS, 1, SLAB_STATE), ab_im=ab_im.reshape(N_SLABS, 1, SLAB_STATE),
        d=s5_d[j].reshape(1, D_MODEL), w_glu=s5_w_glu[j].astype(BF16), b_glu=s5_b_glu[j].reshape(1, D_MODEL),
        w_out=s5_w_out[j].astype(BF16), final_g=final_norm_g.reshape(1, D_MODEL))


def _rope_tables(pos):
    inv_freq = ROPE_THETA ** (-jnp.arange(ROPE_HALF, dtype=F32) / ROPE_HALF)
    ang = pos.astype(F32)[:, None] * inv_freq[None, :]
    cos, sin = jnp.cos(ang), jnp.sin(ang)
    r = pos.shape[0]
    ones = jnp.ones((r, HEAD_DIM - ROPE_DIM), F32)
    zeros = jnp.zeros((r, HEAD_DIM - ROPE_DIM), F32)
    zh = jnp.zeros((r, ROPE_HALF), F32)
    c = jnp.concatenate([cos, cos, ones], axis=1)
    sn = jnp.concatenate([-sin, zh, zeros], axis=1)
    sp = jnp.concatenate([zh, sin, zeros], axis=1)
    tile = lambda z: jnp.concatenate([z] * HEADS_PER_VREG, axis=1)
    return tile(c), tile(sn), tile(sp)


def _inproj_body(*refs, fox, kv_t):
    if fox:
        x_ref, g_ref, w_ref, wf_ref, bf_ref, q_ref, k_ref, v_ref, gate_ref, lf_ref = refs
    else:
        x_ref, g_ref, w_ref, cos_ref, sn_ref, sp_ref, q_ref, k_ref, v_ref, gate_ref = refs
    a = _rmsnorm(x_ref[...], g_ref[...]).astype(BF16)
    proj = _dot(a, w_ref[...])

    def put(dst, c, z):
        if kv_t:
            dst[c * LANES:(c + 1) * LANES, :] = z.T
        else:
            dst[:, c * LANES:(c + 1) * LANES] = z

    nchunk = D_MODEL // LANES
    chunk = lambda base, c: proj[:, base + c * LANES: base + (c + 1) * LANES]
    if fox:
        q_ref[...] = proj[:, :D_MODEL]
        for c in range(nchunk):
            put(k_ref, c, chunk(D_MODEL, c))
        z = _dot_nt(wf_ref[...], a) + bf_ref[...]
        lf_ref[...] = jnp.minimum(z, 0.0) - jnp.log1p(jnp.exp(-jnp.abs(z)))
    else:
        cos, sn, sp = cos_ref[...], sn_ref[...], sp_ref[...]
        rot = lambda z: z * cos + pltpu.roll(z, LANES - ROPE_HALF, 1) * sn + pltpu.roll(z, ROPE_HALF, 1) * sp
        for c in range(nchunk):
            q_ref[:, c * LANES:(c + 1) * LANES] = rot(chunk(0, c))
            put(k_ref, c, rot(chunk(D_MODEL, c)))
    for c in range(nchunk):
        put(v_ref, c, chunk(2 * D_MODEL, c))
    gate_ref[...] = proj[:, 3 * D_MODEL:]


def _inproj(x2d, norm_g, w4, tables=None, wf_t=None, b_f=None, *, tm=256, seq_len=None):
    n = x2d.shape[0]
    fox = wf_t is not None
    kv_t = seq_len is not None
    row = lambda i: (i, 0)
    const = lambda i: (0, 0)
    in_specs = [pl.BlockSpec((tm, D_MODEL), row), pl.BlockSpec((1, D_MODEL), const),
                pl.BlockSpec((D_MODEL, 4 * D_MODEL), const)]
    out_specs = [pl.BlockSpec((tm, D_MODEL), row)] * 4
    out_shape = [jax.ShapeDtypeStruct((n, D_MODEL), F32)] * 4
    if kv_t:
        tps = seq_len // tm
        kv_spec = pl.BlockSpec((None, D_MODEL, tm), lambda i: (i // tps, 0, i % tps))
        kv_shape = jax.ShapeDtypeStruct((n // seq_len, D_MODEL, seq_len), F32)
        out_specs[1:3] = [kv_spec, kv_spec]
        out_shape[1:3] = [kv_shape, kv_shape]
    args = [x2d, norm_g.reshape(1, D_MODEL), w4]
    if fox:
        in_specs += [pl.BlockSpec((N_HEADS, D_MODEL), const), pl.BlockSpec((N_HEADS, 1), const)]
        out_specs += [pl.BlockSpec((N_HEADS, tm), lambda i: (0, i))]
        out_shape += [jax.ShapeDtypeStruct((N_HEADS, n), F32)]
        args += [wf_t, b_f.reshape(N_HEADS, 1)]
    else:
        nrep = tables[0].shape[0] // tm
        tab = lambda i: (i % nrep, 0)
        in_specs += [pl.BlockSpec((tm, LANES), tab)] * 3
        args += list(tables)
    return pl.pallas_call(
        functools.partial(_inproj_body, fox=fox, kv_t=kv_t), grid=(n // tm,), in_specs=in_specs,
        out_specs=tuple(out_specs), out_shape=tuple(out_shape), compiler_params=_cparams(1),
        name="inproj_fox" if fox else "inproj_moba")(*args)


def _outproj_body(o_ref, gate_ref, x_ref, w_ref, y_ref):
    v = (o_ref[...] * _silu(gate_ref[...])).astype(BF16)
    y_ref[...] = x_ref[...] + _dot(v, w_ref[...])


def _outproj(o2d, gate2d, x2d, w, *, tm=512):
    n = x2d.shape[0]
    row = lambda i: (i, 0)
    spec = pl.BlockSpec((tm, D_MODEL), row)
    return pl.pallas_call(
        _outproj_body, grid=(n // tm,),
        in_specs=[spec, spec, spec, pl.BlockSpec((D_MODEL, D_MODEL), lambda i: (0, 0))],
        out_specs=spec, out_shape=jax.ShapeDtypeStruct((n, D_MODEL), F32),
        compiler_params=_cparams(1), name="outproj")(o2d, gate2d, x2d, w)


def _cumsum_body(x_ref, o_ref):
    o_ref[...] = _cumsum_lanes(x_ref[...], x_ref.shape[-1])


def _cumsum(lf_t, t_len):
    h, n = lf_t.shape
    spec = pl.BlockSpec((h, t_len), lambda b: (0, b))
    return pl.pallas_call(_cumsum_body, grid=(n // t_len,), in_specs=[spec], out_specs=spec,
                          out_shape=jax.ShapeDtypeStruct((h, n), F32),
                          compiler_params=_cparams(1), name="cumsum")(lf_t)


def _head_mask(h):
    lane = lax.broadcasted_iota(jnp.int32, (1, LANES), 1)
    return (lane // HEAD_DIM) == h


def _attn_prompt_body(*refs, fox, tq):
    if fox:
        q_ref, k_ref, v_ref, ck_ref, o_ref, kb_s, vt_s, ckb_s = refs
    else:
        q_ref, k_ref, v_ref, o_ref, kb_s, vt_s, km_s = refs
    t_len = k_ref.shape[1]
    nb = t_len // tq

    for i in range(nb):
        k = k_ref[:, i * tq:(i + 1) * tq].T
        kb_s[i * tq:(i + 1) * tq, :] = k.astype(BF16)
        vt_s[:, i * tq:(i + 1) * tq] = v_ref[:, i * tq:(i + 1) * tq].astype(BF16)
        if not fox:
            km_s[i:i + 1, :] = jnp.mean(k, axis=0, keepdims=True)
    if fox:
        for h in range(HEADS_PER_VREG):
            ckb_s[h] = jnp.broadcast_to(ck_ref[:, h:h + 1], (t_len, LANES))

    key_id = lax.broadcasted_iota(jnp.int32, (tq, tq), 0)
    qry_id = lax.broadcasted_iota(jnp.int32, (tq, tq), 1)
    causal = key_id <= qry_id
    blk_id = lax.broadcasted_iota(jnp.int32, (nb, tq), 0)
    for qi in range(nb):
        q = q_ref[qi * tq:(qi + 1) * tq, :]
        o_parts = []
        for h in range(HEADS_PER_VREG):
            hd = slice(h * HEAD_DIM, (h + 1) * HEAD_DIM)
            qm = jnp.where(_head_mask(h), q, 0.0)
            qt = (qm * QK_SCALE).T.astype(BF16)
            sel_bias = None
            if not fox and qi > MOBA_TOPK:
                gate = lax.dot_general(km_s[...], qm, (((1,), (1,)), ((), ())),
                                       precision=lax.Precision.HIGHEST, preferred_element_type=F32)
                gate = jnp.where(blk_id < qi, gate, NEG_INF)
                cnt = jnp.zeros((nb, tq), jnp.int32)
                for i in range(qi):
                    row = gate[i:i + 1, :]
                    beats = (row > gate) | ((row == gate) & (blk_id > i))
                    cnt = cnt + beats.astype(jnp.int32)
                sel_bias = jnp.where(cnt < MOBA_TOPK, 0.0, NEG_INF)

            def scores(i0, n):
                ks = slice(i0 * tq, (i0 + n) * tq)
                s = _dot(kb_s[ks, :], qt)
                if fox:
                    s = s - jnp.concatenate([ckb_s[h, ks, :]] * (tq // LANES), axis=1)
                elif sel_bias is not None and i0 < qi:
                    s = s + jnp.concatenate(
                        [jnp.broadcast_to(sel_bias[i:i + 1, :], (tq, tq)) for i in range(i0, i0 + n)], axis=0)
                return s, ks

            s, ks = scores(qi, 1)
            s = jnp.where(causal, s, NEG_INF)
            m = jnp.max(s, axis=0, keepdims=True)
            p = jnp.exp(s - m)
            l = jnp.sum(p, axis=0, keepdims=True)
            acc = _dot(vt_s[hd, ks], p.astype(BF16))
            for i0 in range(0, qi, KEY_TILES):
                s, ks = scores(i0, min(KEY_TILES, qi - i0))
                m_new = jnp.maximum(m, jnp.max(s, axis=0, keepdims=True))
                alpha = jnp.exp(m - m_new)
                p = jnp.exp(s - m_new)
                l = alpha * l + jnp.sum(p, axis=0, keepdims=True)
                acc = alpha * acc + _dot(vt_s[hd, ks], p.astype(BF16))
                m = m_new
            o_parts.append(acc / l)
        o_ref[qi * tq:(qi + 1) * tq, :] = jnp.concatenate(o_parts, axis=0).T


def _attn_prompt(q2d, k_t, v_t, bn, t_len, ck=None, *, tq=MOBA_BLOCK):
    fox = ck is not None
    nq = t_len // tq
    spec = pl.BlockSpec((t_len, LANES), lambda b, hp: (b, hp))
    kv_spec = pl.BlockSpec((None, LANES, t_len), lambda b, hp: (b, hp, 0))
    in_specs = [spec, kv_spec, kv_spec]
    args = [q2d, k_t, v_t]
    scratch = [pltpu.VMEM((t_len, LANES), BF16), pltpu.VMEM((LANES, t_len), BF16)]
    if fox:
        in_specs.append(pl.BlockSpec((None, None, t_len, HEADS_PER_VREG), lambda b, hp: (b, hp, 0, 0)))
        args.append(ck)
        scratch.append(pltpu.VMEM((HEADS_PER_VREG, t_len, LANES), F32))
    else:
        scratch.append(pltpu.VMEM((nq, LANES), F32))
    return pl.pallas_call(
        functools.partial(_attn_prompt_body, fox=fox, tq=tq), grid=(bn, N_HEAD_PAIRS),
        in_specs=in_specs, out_specs=spec, out_shape=jax.ShapeDtypeStruct(q2d.shape, F32),
        scratch_shapes=scratch, compiler_params=_cparams(2),
        name="attn_prompt_fox" if fox else "attn_prompt_moba")(*args)


def _attn_decode_body(*refs, fox, n_pages, pps, ppb, tn, page):
    k_refs, v_refs = refs[4:4 + pps], refs[4 + pps:4 + 2 * pps]
    if fox:
        lf_refs = refs[4 + 2 * pps:4 + 3 * pps]
        lfn_ref, o_ref, qbd_s, m_s, l_s, r_s, acc_s, pad_s, c_s = refs[4 + 3 * pps:]
    else:
        o_ref, qbd_s, m_s, l_s, r_s, acc_s, pad_s = refs[4 + 2 * pps:]
    q_ref, kn_ref, vn_ref = refs[1:4]
    st = pl.program_id(1)
    n_steps = n_pages // pps
    nblk = n_pages // ppb
    bps = pps // ppb
    nrow = tn * N_HEADS
    row_head = lax.broadcasted_iota(jnp.int32, (nrow, D_MODEL), 0) % N_HEADS
    lane_head = lax.broadcasted_iota(jnp.int32, (nrow, D_MODEL), 1) // HEAD_DIM
    head_mask = row_head == lane_head

    def rep_rows(z):
        return jnp.concatenate([jnp.broadcast_to(z[i:i + 1], (N_HEADS, z.shape[1])) for i in range(tn)], axis=0)

    @pl.when(st == 0)
    def _():
        qbd_s[...] = jnp.where(head_mask, rep_rows(q_ref[...]) * QK_SCALE, 0.0).astype(BF16)
        if fox:
            c_s[...] = jnp.zeros_like(c_s)

    qbd = qbd_s[...]
    for blk in range(bps):
        pages = range(blk * ppb, (blk + 1) * ppb)
        parts = []
        for u in pages:
            s = _dot(qbd, k_refs[u][...].reshape(D_MODEL, page).astype(BF16))
            if fox:
                c_loc = _cumsum_lanes(lf_refs[u][...], page) + c_s[...]
                c_s[...] = jnp.broadcast_to(c_loc[:, page - 1:page], c_s.shape)
                s = s - jnp.concatenate([c_loc] * tn, axis=0)
            parts.append(s)
        s = jnp.concatenate(parts, axis=1)
        idx = st * bps + blk
        if not fox:
            r_s[idx] = jnp.sum(s, axis=1, keepdims=True)
        m = jnp.max(s, axis=1, keepdims=True)
        p = jnp.exp(s - m)
        m_s[idx] = m
        l_s[idx] = jnp.sum(p, axis=1, keepdims=True)
        p = p.astype(BF16)
        acc = None
        for n, u in enumerate(pages):
            part = _dot_nt(p[:, n * page:(n + 1) * page], v_refs[u][...].reshape(D_MODEL, page).astype(BF16))
            acc = part if acc is None else acc + part
        acc_s[idx] = acc

    @pl.when(st == n_steps - 1)
    def _():
        pad_s[...] = jnp.zeros_like(pad_s)
        pad_s[0:tn, :] = kn_ref[...]
        s_new = _dot_nt(qbd, pad_s[...].astype(BF16))
        if fox:
            c_new = _cumsum_lanes(lfn_ref[...], page) + c_s[...]
            s_new = s_new - jnp.concatenate([c_new] * tn, axis=0)
        key_id = lax.broadcasted_iota(jnp.int32, (nrow, page), 1)
        q_id = lax.broadcasted_iota(jnp.int32, (nrow, page), 0) // N_HEADS
        s_new = jnp.where(key_id <= q_id, s_new, NEG_INF)
        pad_s[0:tn, :] = vn_ref[...]
        m_new = jnp.max(s_new, axis=1, keepdims=True)
        p_new = jnp.exp(s_new - m_new)
        l_new = jnp.sum(p_new, axis=1, keepdims=True)
        acc_new = _dot(p_new.astype(BF16), pad_s[...].astype(BF16))

        if fox:
            use = [None] * nblk
        else:
            gates = [r_s[b] for b in range(nblk)]
            use = []
            for b in range(nblk):
                cnt = jnp.zeros((nrow, 1), jnp.int32)
                for b2 in range(nblk):
                    if b2 == b:
                        continue
                    beats = (gates[b2] > gates[b]) | ((gates[b2] == gates[b]) & (b2 < b))
                    cnt = cnt + beats.astype(jnp.int32)
                use.append(cnt < MOBA_TOPK)

        m_tot = m_new
        for i in range(nblk):
            mi = m_s[i] if use[i] is None else jnp.where(use[i], m_s[i], NEG_INF)
            m_tot = jnp.maximum(m_tot, mi)
        w_new = jnp.exp(m_new - m_tot)
        l_tot = w_new * l_new
        acc = w_new * acc_new
        for i in range(nblk):
            w = jnp.exp(m_s[i] - m_tot)
            if use[i] is not None:
                w = jnp.where(use[i], w, 0.0)
            l_tot = l_tot + w * l_s[i]
            acc = acc + w * acc_s[i]
        o_full = jnp.where(head_mask, acc / l_tot, 0.0)
        for i in range(tn):
            o_ref[i:i + 1, :] = jnp.sum(o_full[i * N_HEADS:(i + 1) * N_HEADS, :], axis=0, keepdims=True)


def _attn_decode(q3, kn3, vn3, k_cache_t, v_cache_t, layer, page_table, lf_cache_t=None, lf_new=None):
    fox = lf_cache_t is not None
    sn, tn, _ = q3.shape
    n_pages = page_table.shape[1]
    page = k_cache_t.shape[-1]
    ppb = MOBA_BLOCK // page
    pps = DECODE_PAGES_PER_STEP
    assert pps % ppb == 0 and n_pages % pps == 0
    nblk = n_pages // ppb
    nrow = tn * N_HEADS
    seq = lambda s, st, pt: (s, 0, 0)
    new_spec = pl.BlockSpec((None, tn, D_MODEL), seq)

    def cache_spec(u):
        return pl.BlockSpec((None, None, N_HEADS, HEAD_DIM, page),
                            lambda s, st, pt: (layer, pt[s * n_pages + st * pps + u], 0, 0, 0))

    def lf_spec(u):
        return pl.BlockSpec((None, N_HEADS, page), lambda s, st, pt: (pt[s * n_pages + st * pps + u], 0, 0))

    in_specs = [new_spec, new_spec, new_spec] + [cache_spec(u) for u in range(pps)] * 2
    args = [q3, kn3, vn3] + [k_cache_t] * pps + [v_cache_t] * pps
    scratch = [pltpu.VMEM((nrow, D_MODEL), BF16),
               pltpu.VMEM((nblk, nrow, 1), F32), pltpu.VMEM((nblk, nrow, 1), F32),
               pltpu.VMEM((nblk, nrow, 1), F32), pltpu.VMEM((nblk, nrow, D_MODEL), F32),
               pltpu.VMEM((page, D_MODEL), F32)]
    if fox:
        in_specs += [lf_spec(u) for u in range(pps)] + [pl.BlockSpec((None, N_HEADS, page), seq)]
        args += [lf_cache_t] * pps + [lf_new]
        scratch.append(pltpu.VMEM((N_HEADS, page), F32))
    grid_spec = pltpu.PrefetchScalarGridSpec(
        num_scalar_prefetch=1, grid=(sn, n_pages // pps), in_specs=in_specs, out_specs=new_spec,
        scratch_shapes=scratch)
    return pl.pallas_call(
        functools.partial(_attn_decode_body, fox=fox, n_pages=n_pages, pps=pps, ppb=ppb, tn=tn, page=page),
        grid_spec=grid_spec, out_shape=jax.ShapeDtypeStruct(q3.shape, F32),
        compiler_params=_cparams(2),
        name="attn_decode_fox" if fox else "attn_decode_moba")(page_table.reshape(-1), *args)


def kernel(x_prompt, x_sample, state_s5_re, state_s5_im, cache_moba_k, cache_moba_v, cache_fox_k, cache_fox_v, cache_fox_logf, page_table, norm_g, final_norm_g, s5_w_in, s5_lam_re, s5_lam_im, s5_b_re, s5_b_im, s5_c_re, s5_c_im, s5_d, s5_log_dt, s5_w_glu, s5_b_glu, s5_w_out, moba_w_in, moba_w_out, fox_w_in, fox_b_f, fox_w_out):
    bn, t_len, _ = x_prompt.shape
    sn, tn, _ = x_sample.shape
    depth = norm_g.shape[0]
    n_pool, page = cache_moba_k.shape[1], cache_moba_k.shape[2]
    past_len = page_table.shape[1] * page
    nstate = S5_GROUPS * S5_STATE
    att = N_HEADS * HEAD_DIM

    tab_p = _rope_tables(jnp.arange(t_len, dtype=jnp.int32))
    pos_s = past_len + jnp.arange(tn, dtype=jnp.int32)
    tab_s = tuple(jnp.tile(z, (sn, 1)) for z in _rope_tables(pos_s))

    hp, hs = x_prompt, x_sample
    outs = {k: [] for k in ("s5r_p", "s5i_p", "s5r_s", "s5i_s", "mk_p", "mv_p", "mk_s", "mv_s",
                            "fk_p", "fv_p", "fl_p", "fk_s", "fv_s", "fl_s")}
    zero_state = jnp.zeros((bn, nstate), F32)
    for i in range(depth):
        kind, j = i % 3, i // 3
        last = i == depth - 1
        if kind == 0:
            prm = _s5_params(j, norm_g[i], final_norm_g, s5_w_in, s5_lam_re, s5_lam_im, s5_b_re, s5_b_im,
                             s5_c_re, s5_c_im, s5_d, s5_log_dt, s5_w_glu, s5_b_glu, s5_w_out)
            hp, re_p, im_p = _s5_layer(hp, zero_state, zero_state, prm, time_major=False, tt=32, final_norm=last)
            hs_t, re_s, im_s = _s5_layer(hs.transpose(1, 0, 2), state_s5_re[j].reshape(sn, nstate),
                                         state_s5_im[j].reshape(sn, nstate), prm,
                                         time_major=True, tt=tn, final_norm=last)
            hs = hs_t.transpose(1, 0, 2)
            outs["s5r_p"].append(re_p.reshape(bn, S5_GROUPS, S5_STATE))
            outs["s5i_p"].append(im_p.reshape(bn, S5_GROUPS, S5_STATE))
            outs["s5r_s"].append(re_s.reshape(sn, S5_GROUPS, S5_STATE))
            outs["s5i_s"].append(im_s.reshape(sn, S5_GROUPS, S5_STATE))
            continue

        fox = kind == 2
        w_in = (fox_w_in if fox else moba_w_in)[j]
        w_out = (fox_w_out if fox else moba_w_out)[j].astype(BF16)
        w4 = w_in[:, :4 * att].astype(BF16)
        xp2, xs2 = hp.reshape(bn * t_len, D_MODEL), hs.reshape(sn * tn, D_MODEL)
        if fox:
            wf_t = w_in[:, 4 * att:].T.astype(BF16)
            res_p = _inproj(xp2, norm_g[i], w4, wf_t=wf_t, b_f=fox_b_f[j], seq_len=t_len)
            res_s = _inproj(xs2, norm_g[i], w4, wf_t=wf_t, b_f=fox_b_f[j])
        else:
            res_p = _inproj(xp2, norm_g[i], w4, tab_p, seq_len=t_len)
            res_s = _inproj(xs2, norm_g[i], w4, tab_s)
        q_p, k_p, v_p, g_p = res_p[:4]
        q_s, k_s, v_s, g_s = res_s[:4]
        kc = (cache_fox_k if fox else cache_moba_k).transpose(0, 1, 3, 4, 2)
        vc = (cache_fox_v if fox else cache_moba_v).transpose(0, 1, 3, 4, 2)
        three = lambda z: z.reshape(sn, tn, D_MODEL)
        heads_last = lambda z: z.reshape(bn, N_HEADS, HEAD_DIM, t_len).transpose(0, 3, 1, 2)
        if fox:
            lf_p, lf_s = res_p[4], res_s[4]
            ck = _cumsum(lf_p, t_len).reshape(N_HEAD_PAIRS, HEADS_PER_VREG, bn, t_len).transpose(2, 0, 3, 1)
            o_p = _attn_prompt(q_p, k_p, v_p, bn, t_len, ck)
            lf_cache_t = cache_fox_logf[j].transpose(0, 2, 1)
            lf_new = jnp.pad(lf_s.reshape(N_HEADS, sn, tn).transpose(1, 0, 2), ((0, 0), (0, 0), (0, page - tn)))
            o_s = _attn_decode(three(q_s), three(k_s), three(v_s), kc, vc, j, page_table, lf_cache_t, lf_new)
            outs["fk_p"].append(heads_last(k_p))
            outs["fv_p"].append(heads_last(v_p))
            outs["fl_p"].append(lf_p.reshape(N_HEADS, bn, t_len).transpose(1, 2, 0))
            outs["fk_s"].append(k_s.reshape(sn, tn, N_HEADS, HEAD_DIM))
            outs["fv_s"].append(v_s.reshape(sn, tn, N_HEADS, HEAD_DIM))
            outs["fl_s"].append(lf_s.reshape(N_HEADS, sn, tn).transpose(1, 2, 0))
        else:
            o_p = _attn_prompt(q_p, k_p, v_p, bn, t_len)
            o_s = _attn_decode(three(q_s), three(k_s), three(v_s), kc, vc, j, page_table)
            outs["mk_p"].append(heads_last(k_p))
            outs["mv_p"].append(heads_last(v_p))
            outs["mk_s"].append(k_s.reshape(sn, tn, N_HEADS, HEAD_DIM))
            outs["mv_s"].append(v_s.reshape(sn, tn, N_HEADS, HEAD_DIM))
        hp = _outproj(o_p, g_p, xp2, w_out).reshape(bn, t_len, D_MODEL)
        hs = _outproj(o_s.reshape(sn * tn, D_MODEL), g_s, xs2, w_out).reshape(sn, tn, D_MODEL)

    st = lambda k: jnp.stack(outs[k])
    return (hp, hs, st("s5r_p"), st("s5i_p"), st("mk_p"), st("mv_p"), st("fk_p"), st("fv_p"), st("fl_p"),
            st("s5r_s"), st("s5i_s"), st("mk_s"), st("mv_s"), st("fk_s"), st("fv_s"), st("fl_s"))
```
